```python
import math
import jax, jax.numpy as jnp
from jax import lax
import numpy as np

D_MODEL = 1024
BATCH = 16
SEQ = 2048
DEPTH = 2

N_HEADS = 16
HEAD_DIM = D_MODEL // N_HEADS
N_MIXERS = 2
CHUNK = 64
LEFT_CHUNKS = 8
BAND = (LEFT_CHUNKS + 1) * CHUNK
REL_CLIP = 128
N_REL = (CHUNK - 1) + REL_CLIP + 1
Q_BLOCK = 128
D_FF = int(math.ceil(D_MODEL * 8 / 3 / 256) * 256)
N_A_LAYERS = (DEPTH + N_MIXERS - 1) // N_MIXERS
RMS_EPS = 1e-6

kernel_name = "hybrid_chunkattn_stickbreak_trunk"


def rms_norm(x, g):
    xf = x.astype(jnp.float32)
    y = xf * lax.rsqrt(jnp.mean(xf * xf, axis=-1, keepdims=True) + RMS_EPS)
    return (y * g.astype(jnp.float32)).astype(x.dtype)


def split_heads(h, w_qkv):
    b, s, _ = h.shape
    qkv = (h @ w_qkv).reshape(b, s, 3, N_HEADS, HEAD_DIM)
    qkv = jnp.transpose(qkv, (2, 0, 3, 1, 4))
    return qkv[0], qkv[1], qkv[2]


def merge_heads(o):
    b, h, s, d = o.shape
    return jnp.transpose(o, (0, 2, 1, 3)).reshape(b, s, h * d)


def chunk_relpos_attention(h, w_qkv, w_o, rel_bias):
    b, s, _ = h.shape
    n_chunks = s // CHUNK
    q, k, v = split_heads(h, w_qkv)
    i_idx = np.arange(CHUNK)[:, None]
    j_idx = np.arange(BAND)[None, :]
    rel = np.clip(LEFT_CHUNKS * CHUNK + i_idx - j_idx, -(CHUNK - 1), REL_CLIP) + (CHUNK - 1)
    bias = jnp.take(rel_bias, jnp.asarray(rel, dtype=jnp.int32), axis=1).astype(jnp.float32)
    scale = 1.0 / math.sqrt(HEAD_DIM)
    outs = []
    for c in range(n_chunks):
        k0 = max(0, (c - LEFT_CHUNKS) * CHUNK)
        k1 = (c + 1) * CHUNK
        n_keys = k1 - k0
        q_blk = q[:, :, c * CHUNK:(c + 1) * CHUNK]
        sc = jnp.einsum('bhqd,bhkd->bhqk', q_blk, k[:, :, k0:k1]).astype(jnp.float32) * scale
        sc = sc + bias[None, :, :, BAND - n_keys:]
        p = jax.nn.softmax(sc, axis=-1).astype(v.dtype)
        outs.append(jnp.einsum('bhqk,bhkd->bhqd', p, v[:, :, k0:k1]))
    o = jnp.concatenate(outs, axis=2)
    return merge_heads(o) @ w_o


def stick_breaking_attention(h, w_qkv, w_o):
    b, s, _ = h.shape
    q, k, v = split_heads(h, w_qkv)
    scale = 1.0 / math.sqrt(HEAD_DIM)
    outs = []
    for qb in range(s // Q_BLOCK):
        n_keys = (qb + 1) * Q_BLOCK
        q_blk = q[:, :, qb * Q_BLOCK:(qb + 1) * Q_BLOCK]
        z = jnp.einsum('bhqd,bhkd->bhqk', q_blk, k[:, :, :n_keys]).astype(jnp.float32) * scale
        t_pos = qb * Q_BLOCK + np.arange(Q_BLOCK)
        causal = jnp.asarray((np.arange(n_keys)[None, :] < t_pos[:, None])[None, None])
        log_keep = jnp.where(causal, jax.nn.log_sigmoid(-z), 0.0)
        suffix = lax.cumsum(log_keep, axis=3, reverse=True) - log_keep
        a = jnp.where(causal, jnp.exp(jax.nn.log_sigmoid(z) + suffix), 0.0).astype(v.dtype)
        outs.append(jnp.einsum('bhqk,bhkd->bhqd', a, v[:, :, :n_keys]))
    o = jnp.concatenate(outs, axis=2)
    return merge_heads(o) @ w_o


def swiglu(h, w_gate, w_up, w_down):
    return (jax.nn.silu(h @ w_gate) * (h @ w_up)) @ w_down


def setup_inputs(seed: int = 0) -> dict:
    key = jax.random.key(seed)
    ks = jax.random.split(key, 12)
    d, f = D_MODEL, D_FF
    nrm = lambda k, shape, fan: jax.random.normal(k, shape, jnp.float32) * (fan ** -0.5)
    gain = lambda k: 1.0 + 0.02 * jax.random.normal(k, (DEPTH, d), jnp.float32)
    return {
        "x": jax.random.normal(ks[0], (BATCH, SEQ, d), jnp.float32),
        "g_pre_mix": gain(ks[1]),
        "g_post_mix": gain(ks[2]),
        "w_qkv": nrm(ks[3], (DEPTH, d, 3 * d), d),
        "w_o": nrm(ks[4], (DEPTH, d, d), d),
        "rel_bias": 0.5 * jax.random.normal(ks[5], (N_A_LAYERS, N_HEADS, N_REL), jnp.float32),
        "g_pre_ffn": gain(ks[6]),
        "g_post_ffn": gain(ks[7]),
        "w_gate": nrm(ks[8], (DEPTH, d, f), d),
        "w_up": nrm(ks[9], (DEPTH, d, f), d),
        "w_down": nrm(ks[10], (DEPTH, f, d), f),
    }


def reference(x, g_pre_mix, g_post_mix, w_qkv, w_o, rel_bias, g_pre_ffn, g_post_ffn, w_gate, w_up, w_down):
    for i in range(DEPTH):
        h = rms_norm(x, g_pre_mix[i])
        if i % N_MIXERS == 0:
            m = chunk_relpos_attention(h, w_qkv[i], w_o[i], rel_bias[i // N_MIXERS])
        else:
            m = stick_breaking_attention(h, w_qkv[i], w_o[i])
        x = x + rms_norm(m, g_post_mix[i])
        h = rms_norm(x, g_pre_ffn[i])
        x = x + rms_norm(swiglu(h, w_gate[i], w_up[i], w_down[i]), g_post_ffn[i])
    return x
```

```python
import functools
import math

import jax
import jax.numpy as jnp
from jax import lax
from jax.experimental import pallas as pl
from jax.experimental.pallas import tpu as pltpu

HEAD_DIM = 64
CHUNK = 64
LEFT_CHUNKS = 8
REL_CLIP = 128
RMS_EPS = 1e-6
LOG2E = math.log2(math.e)
MASKED_LOGIT = -1e30

V7X_LANES = 128
V7X_MXU_DIM = 256
V7X_VMEM_BYTES = 64 * 1024 * 1024

HEADS_PER_STEP = V7X_MXU_DIM // HEAD_DIM
ATTN_ROWS = V7X_MXU_DIM
TOKEN_ROWS = 256

_F32 = jnp.float32
_BF16 = jnp.bfloat16
_CONTRACT_LAST = (((1,), (1,)), ((), ()))


def _vmem_limit(estimate_bytes):
    return int(min(estimate_bytes * 3 // 2, V7X_VMEM_BYTES * 15 // 16))


def _rms_norm(x, g):
    ms = jnp.mean(x * x, axis=-1, keepdims=True)
    return x * lax.rsqrt(ms + RMS_EPS) * g


def _resident(shape):
    return pl.BlockSpec(shape, lambda *_: (0,) * len(shape), pipeline_mode=pl.Buffered(1))


def _qkv_body(x_ref, g_ref, w_ref, s_ref, o_ref):
    h = _rms_norm(x_ref[...], g_ref[...]).astype(_BF16)
    y = jnp.dot(h, w_ref[...], preferred_element_type=_F32)
    o_ref[...] = (y * s_ref[...]).astype(_BF16)


def _qkv_proj(x2, g, w, col_scale):
    t, d = x2.shape
    n = w.shape[1]
    tm = TOKEN_ROWS
    est = 2 * tm * d * 4 + d * n * 2 + 2 * tm * n * 2 + tm * n * 4 + tm * d * 6
    return pl.pallas_call(
        _qkv_body,
        grid=(t // tm,),
        in_specs=[
            pl.BlockSpec((tm, d), lambda i: (i, 0)),
            _resident((1, d)),
            _resident((d, n)),
            _resident((1, n)),
        ],
        out_specs=pl.BlockSpec((tm, n), lambda i: (i, 0)),
        out_shape=jax.ShapeDtypeStruct((t, n), _BF16),
        compiler_params=pltpu.CompilerParams(
            dimension_semantics=("arbitrary",), vmem_limit_bytes=_vmem_limit(est)),
        name="qkv_proj",
    )(x2, g, w, col_scale)


def _head_lanes(width, hh):
    lane = lax.broadcasted_iota(jnp.int32, (1, width), 1)
    return (lane >= hh * HEAD_DIM) & (lane < (hh + 1) * HEAD_DIM)


def _chunk_attn_body(rel_ref, q_ref, k_ref, v_ref, o_ref, bias_ref, *, n_rel):
    hg, b, qi = pl.program_id(0), pl.program_id(1), pl.program_id(2)
    tq = ATTN_ROWS
    width = HEADS_PER_STEP * HEAD_DIM
    left = LEFT_CHUNKS * CHUNK
    left_blocks = left // tq
    band_blocks = left_blocks + 1
    master_blocks = band_blocks + left_blocks
    master = master_blocks * tq
    roll_w = master + tq

    @pl.when((b == 0) & (qi == 0))
    def _build_bias():
        wpos = lax.broadcasted_iota(jnp.int32, (8, roll_w), 1)
        ridx = jnp.clip(left + tq - wpos, -(CHUNK - 1), REL_CLIP) + (CHUNK - 1)
        i = lax.broadcasted_iota(jnp.int32, (tq, master), 0)
        u = lax.broadcasted_iota(jnp.int32, (tq, master), 1)
        d = left + i - u
        i_in_chunk = i % CHUNK
        visible = (d >= i_in_chunk - (CHUNK - 1)) & (d <= i_in_chunk + left)
        for hh in range(HEADS_PER_STEP):
            h = hg * HEADS_PER_STEP + hh

            def pick(r, base, h=h):
                return jnp.where(ridx == r, rel_ref[h, r], base)

            base = lax.fori_loop(0, n_rel, pick, jnp.zeros((8, roll_w), _F32))
            rows = jnp.broadcast_to(base[0:1, :] * LOG2E, (tq, roll_w))
            toeplitz = pltpu.roll(rows, master, 1, stride=1, stride_axis=0)
            tile = jnp.where(visible, toeplitz[:, :master], MASKED_LOGIT)
            for ub in range(master_blocks):
                bias_ref[hh, ub] = tile[:, ub * tq:(ub + 1) * tq]

    start_blk = jnp.maximum(qi - left_blocks, 0)
    bias_blk0 = left_blocks - (qi - start_blk)
    start = pl.multiple_of(start_blk * tq, tq)
    kb = k_ref[pl.ds(start, band_blocks * tq), :]
    vb = v_ref[pl.ds(start, band_blocks * tq), :]
    q = q_ref[...]
    out = None
    for hh in range(HEADS_PER_STEP):
        qm = jnp.where(_head_lanes(width, hh), q, jnp.zeros_like(q))
        z = lax.dot_general(qm, kb, _CONTRACT_LAST, preferred_element_type=_F32)
        z = z + jnp.concatenate(
            [bias_ref[hh, bias_blk0 + t] for t in range(band_blocks)], axis=1)
        p = jnp.exp2(z - jnp.max(z, axis=-1, keepdims=True))
        denom = jnp.sum(p, axis=-1, keepdims=True)
        o = jnp.dot(p.astype(_BF16), vb, preferred_element_type=_F32) / denom
        if out is None:
            out = o
        else:
            lane = lax.broadcasted_iota(jnp.int32, (1, width), 1)
            out = jnp.where(lane >= hh * HEAD_DIM, o, out)
    o_ref[...] = out.astype(_BF16)


def _attn_specs(seq, d):
    tq = ATTN_ROWS
    width = HEADS_PER_STEP * HEAD_DIM
    nq = seq // tq
    col_blocks = d // width
    q_spec = pl.BlockSpec((tq, width), lambda hg, b, qi: (b * nq + qi, hg))
    k_spec = pl.BlockSpec((seq, width), lambda hg, b, qi: (b, col_blocks + hg))
    v_spec = pl.BlockSpec((seq, width), lambda hg, b, qi: (b, 2 * col_blocks + hg))
    o_spec = pl.BlockSpec((tq, width), lambda hg, b, qi: (b * nq + qi, hg))
    return q_spec, k_spec, v_spec, o_spec, (col_blocks, None, nq)


def _chunk_attention(qkv, rel_bias, batch, seq):
    t, d3 = qkv.shape
    d = d3 // 3
    tq = ATTN_ROWS
    width = HEADS_PER_STEP * HEAD_DIM
    left_blocks = LEFT_CHUNKS * CHUNK // tq
    master_blocks = 2 * left_blocks + 1
    q_spec, k_spec, v_spec, o_spec, (hgs, _, nq) = _attn_specs(seq, d)
    bias_bytes = HEADS_PER_STEP * master_blocks * tq * tq * 4
    est = bias_bytes + 4 * seq * width * 2 + 4 * tq * width * 2 + 8 * tq * 3 * tq * 4
    return pl.pallas_call(
        functools.partial(_chunk_attn_body, n_rel=rel_bias.shape[1]),
        grid=(hgs, batch, nq),
        in_specs=[pl.BlockSpec(memory_space=pltpu.SMEM), q_spec, k_spec, v_spec],
        out_specs=o_spec,
        out_shape=jax.ShapeDtypeStruct((t, d), _BF16),
        scratch_shapes=[pltpu.VMEM((HEADS_PER_STEP, master_blocks, tq, tq), _F32)],
        compiler_params=pltpu.CompilerParams(
            dimension_semantics=("arbitrary", "arbitrary", "arbitrary"),
            vmem_limit_bytes=_vmem_limit(est)),
        name="chunk_attention",
    )(rel_bias, qkv, qkv, qkv)


def _stick_attn_body(q_ref, k_ref, v_ref, o_ref, qs_ref, acc_ref, carry_ref):
    qi = pl.program_id(2)
    tq = ATTN_ROWS
    nh = HEADS_PER_STEP
    width = nh * HEAD_DIM

    q = q_ref[...]
    for hh in range(nh):
        qs_ref[hh * tq:(hh + 1) * tq, :] = jnp.where(_head_lanes(width, hh), q, jnp.zeros_like(q))
    acc_ref[...] = jnp.zeros_like(acc_ref)
    carry_ref[...] = jnp.zeros_like(carry_ref)

    r2 = lax.broadcasted_iota(jnp.int32, (2 * tq, tq), 0)
    c2 = lax.broadcasted_iota(jnp.int32, (2 * tq, tq), 1)
    neg_suffix = jnp.where(r2 % tq >= c2, -1.0, 0.0).astype(_BF16)

    def block(j, diag):
        off = pl.multiple_of(j * tq, tq)
        kj = k_ref[pl.ds(off, tq), :]
        vj = v_ref[pl.ds(off, tq), :]
        z = lax.dot_general(qs_ref[...], kj, _CONTRACT_LAST, preferred_element_type=_F32)
        sp = jnp.maximum(z, 0.0) + jnp.log2(1.0 + jnp.exp2(-jnp.abs(z)))
        if diag:
            row = lax.broadcasted_iota(jnp.int32, z.shape, 0) % tq
            col = lax.broadcasted_iota(jnp.int32, z.shape, 1)
            causal = col < row
            sp = jnp.where(causal, sp, 0.0)
        hi = sp.astype(_BF16)
        lo = (sp - hi.astype(_F32)).astype(_BF16)
        cin = jnp.dot(jnp.concatenate([hi, lo], axis=1), neg_suffix, preferred_element_type=_F32)
        carry = carry_ref[...]
        logw = z + cin + jnp.concatenate([carry] * (tq // V7X_LANES), axis=1)
        a = jnp.exp2(logw)
        if diag:
            a = jnp.where(causal, a, 0.0)
        a = a.astype(_BF16)
        a_cat = jnp.concatenate([a[hh * tq:(hh + 1) * tq] for hh in range(nh)], axis=1)
        v_cat = jnp.concatenate(
            [jnp.where(_head_lanes(width, hh), vj, jnp.zeros_like(vj)) for hh in range(nh)], axis=0)
        acc_ref[...] += jnp.dot(a_cat, v_cat, preferred_element_type=_F32)
        carry_ref[...] = carry + jnp.broadcast_to(cin[:, 0:1], carry.shape)

    block(qi, True)

    def earlier(step, c):
        block(qi - 1 - step, False)
        return c

    lax.fori_loop(0, qi, earlier, 0)
    o_ref[...] = acc_ref[...].astype(_BF16)


def _stick_attention(qkv, batch, seq):
    t, d3 = qkv.shape
    d = d3 // 3
    tq = ATTN_ROWS
    nh = HEADS_PER_STEP
    width = nh * HEAD_DIM
    q_spec, k_spec, v_spec, o_spec, (hgs, _, nq) = _attn_specs(seq, d)
    est = (4 * seq * width * 2 + 4 * tq * width * 2 + nh * tq * width * 2 + tq * width * 4
           + nh * tq * V7X_LANES * 4 + 10 * nh * tq * tq * 4)
    return pl.pallas_call(
        _stick_attn_body,
        grid=(hgs, batch, nq),
        in_specs=[q_spec, k_spec, v_spec],
        out_specs=o_spec,
        out_shape=jax.ShapeDtypeStruct((t, d), _BF16),
        scratch_shapes=[
            pltpu.VMEM((nh * tq, width), _BF16),
            pltpu.VMEM((tq, width), _F32),
            pltpu.VMEM((nh * tq, V7X_LANES), _F32),
        ],
        compiler_params=pltpu.CompilerParams(
            dimension_semantics=("arbitrary", "arbitrary", "arbitrary"),
            vmem_limit_bytes=_vmem_limit(est)),
        name="stick_attention",
    )(qkv, qkv, qkv)


def _post_body(o_ref, x_ref, wo_ref, g_post_mix_ref, g_pre_ffn_ref, wg_ref, wu_ref, wd_ref,
               g_post_ffn_ref, out_ref):
    m = jnp.dot(o_ref[...], wo_ref[...], preferred_element_type=_F32)
    x1 = x_ref[...] + _rms_norm(m, g_post_mix_ref[...])
    h = _rms_norm(x1, g_pre_ffn_ref[...]).astype(_BF16)
    gate = jnp.dot(h, wg_ref[...], preferred_element_type=_F32)
    up = jnp.dot(h, wu_ref[...], preferred_element_type=_F32)
    act = (gate * jax.nn.sigmoid(gate) * up).astype(_BF16)
    y = jnp.dot(act, wd_ref[...], preferred_element_type=_F32)
    out_ref[...] = x1 + _rms_norm(y, g_post_ffn_ref[...])


def _post_mix_ffn(o, x2, wo, g_post_mix, g_pre_ffn, wg, wu, wd, g_post_ffn):
    t, d = x2.shape
    f = wg.shape[1]
    tm = TOKEN_ROWS
    est = (d * d + 3 * d * f) * 2 + 2 * tm * d * (2 + 4 + 4) + tm * (3 * d * 4 + 2 * f * 4 + f * 2 + d * 2)
    row = lambda i: (i, 0)
    return pl.pallas_call(
        _post_body,
        grid=(t // tm,),
        in_specs=[
            pl.BlockSpec((tm, d), row),
            pl.BlockSpec((tm, d), row),
            _resident((d, d)),
            _resident((1, d)),
            _resident((1, d)),
            _resident((d, f)),
            _resident((d, f)),
            _resident((f, d)),
            _resident((1, d)),
        ],
        out_specs=pl.BlockSpec((tm, d), row),
        out_shape=jax.ShapeDtypeStruct((t, d), _F32),
        compiler_params=pltpu.CompilerParams(
            dimension_semantics=("arbitrary",), vmem_limit_bytes=_vmem_limit(est)),
        name="post_mix_ffn",
    )(o, x2, wo, g_post_mix, g_pre_ffn, wg, wu, wd, g_post_ffn)


def kernel(x, g_pre_mix, g_post_mix, w_qkv, w_o, rel_bias, g_pre_ffn, g_post_ffn, w_gate, w_up, w_down):
    batch, seq, d = x.shape
    depth = w_qkv.shape[0]
    assert d % (HEADS_PER_STEP * HEAD_DIM) == 0 and seq % ATTN_ROWS == 0
    assert (LEFT_CHUNKS * CHUNK) % ATTN_ROWS == 0 and seq >= LEFT_CHUNKS * CHUNK + ATTN_ROWS
    assert (batch * seq) % TOKEN_ROWS == 0

    q_scale = LOG2E / math.sqrt(HEAD_DIM)
    col_scale = jnp.concatenate(
        [jnp.full((1, d), q_scale, _F32), jnp.ones((1, 2 * d), _F32)], axis=1)
    row = lambda g: g.reshape(1, d).astype(_F32)

    x2 = x.reshape(batch * seq, d)
    for i in range(depth):
        qkv = _qkv_proj(x2, row(g_pre_mix[i]), w_qkv[i].astype(_BF16), col_scale)
        if i % 2 == 0:
            o = _chunk_attention(qkv, rel_bias[i // 2].astype(_F32), batch, seq)
        else:
            o = _stick_attention(qkv, batch, seq)
        x2 = _post_mix_ffn(
            o, x2, w_o[i].astype(_BF16), row(g_post_mix[i]), row(g_pre_ffn[i]),
            w_gate[i].astype(_BF16), w_up[i].astype(_BF16), w_down[i].astype(_BF16),
            row(g_post_ffn[i]))
    return x2.reshape(batch, seq, d)
```

```python
import functools
import math

import jax
import jax.numpy as jnp
from jax import lax
from jax.experimental import pallas as pl
from jax.experimental.pallas import tpu as pltpu

HEAD_DIM = 64
CHUNK = 64
LEFT_CHUNKS = 8
REL_CLIP = 128
RMS_EPS = 1e-6
LOG2E = math.log2(math.e)
MASKED_LOGIT = -1e30
DEAD_LOG2_WEIGHT = -160.0

V7X_LANES = 128
V7X_MXU_DIM = 256
V7X_VMEM_BYTES = 64 * 1024 * 1024

HEADS_PER_STEP = V7X_MXU_DIM // HEAD_DIM
ATTN_ROWS = V7X_MXU_DIM
TOKEN_ROWS = 256

_F32 = jnp.float32
_BF16 = jnp.bfloat16
_CONTRACT_LAST = (((1,), (1,)), ((), ()))


def _vmem_limit(estimate_bytes):
    return int(min(estimate_bytes * 3 // 2, V7X_VMEM_BYTES * 15 // 16))


def _rms_norm(x, g):
    ms = jnp.mean(x * x, axis=-1, keepdims=True)
    return x * lax.rsqrt(ms + RMS_EPS) * g


def _resident(shape):
    return pl.BlockSpec(shape, lambda *_: (0,) * len(shape), pipeline_mode=pl.Buffered(1))


def _qkv_body(x_ref, g_ref, w_ref, s_ref, o_ref):
    h = _rms_norm(x_ref[...], g_ref[...]).astype(_BF16)
    y = jnp.dot(h, w_ref[...], preferred_element_type=_F32)
    o_ref[...] = (y * s_ref[...]).astype(_BF16)


def _qkv_proj(x2, g, w, col_scale):
    t, d = x2.shape
    n = w.shape[1]
    tm = TOKEN_ROWS
    est = 2 * tm * d * 4 + d * n * 2 + 2 * tm * n * 2 + tm * n * 4 + tm * d * 6
    return pl.pallas_call(
        _qkv_body,
        grid=(t // tm,),
        in_specs=[
            pl.BlockSpec((tm, d), lambda i: (i, 0)),
            _resident((1, d)),
            _resident((d, n)),
            _resident((1, n)),
        ],
        out_specs=pl.BlockSpec((tm, n), lambda i: (i, 0)),
        out_shape=jax.ShapeDtypeStruct((t, n), _BF16),
        compiler_params=pltpu.CompilerParams(
            dimension_semantics=("arbitrary",), vmem_limit_bytes=_vmem_limit(est)),
        name="qkv_proj",
    )(x2, g, w, col_scale)


def _head_lanes(width, hh):
    lane = lax.broadcasted_iota(jnp.int32, (1, width), 1)
    return (lane >= hh * HEAD_DIM) & (lane < (hh + 1) * HEAD_DIM)


def _attn_specs(seq, d):
    width = HEADS_PER_STEP * HEAD_DIM
    col_blocks = d // width
    q_spec = pl.BlockSpec((seq, width), lambda hg, b: (b, hg))
    k_spec = pl.BlockSpec((seq, width), lambda hg, b: (b, col_blocks + hg))
    v_spec = pl.BlockSpec((seq, width), lambda hg, b: (b, 2 * col_blocks + hg))
    o_spec = pl.BlockSpec((seq, width), lambda hg, b: (b, hg))
    return q_spec, k_spec, v_spec, o_spec, col_blocks


def _chunk_attn_body(rel_ref, q_ref, k_ref, v_ref, o_ref, bias_ref, *, n_rel, nq):
    hg, b = pl.program_id(0), pl.program_id(1)
    tq = ATTN_ROWS
    width = HEADS_PER_STEP * HEAD_DIM
    left = LEFT_CHUNKS * CHUNK
    left_blocks = left // tq
    band_blocks = left_blocks + 1
    master_blocks = band_blocks + left_blocks
    master = master_blocks * tq
    roll_w = master + tq

    @pl.when(b == 0)
    def _build_bias():
        wpos = lax.broadcasted_iota(jnp.int32, (8, roll_w), 1)
        ridx = jnp.clip(left + tq - wpos, -(CHUNK - 1), REL_CLIP) + (CHUNK - 1)
        i = lax.broadcasted_iota(jnp.int32, (tq, master), 0)
        u = lax.broadcasted_iota(jnp.int32, (tq, master), 1)
        d = left + i - u
        i_in_chunk = i % CHUNK
        visible = (d >= i_in_chunk - (CHUNK - 1)) & (d <= i_in_chunk + left)
        for hh in range(HEADS_PER_STEP):
            h = hg * HEADS_PER_STEP + hh

            def pick(r, base, h=h):
                return jnp.where(ridx == r, rel_ref[h, r], base)

            base = lax.fori_loop(0, n_rel, pick, jnp.zeros((8, roll_w), _F32))
            rows = jnp.broadcast_to(base[0:1, :] * LOG2E, (tq, roll_w))
            toeplitz = pltpu.roll(rows, master, 1, stride=1, stride_axis=0)
            tile = jnp.where(visible, toeplitz[:, :master], MASKED_LOGIT)
            for ub in range(master_blocks):
                bias_ref[hh, ub] = tile[:, ub * tq:(ub + 1) * tq]

    def query_block(qi, c):
        start_blk = jnp.maximum(qi - left_blocks, 0)
        bias_blk0 = left_blocks - (qi - start_blk)
        start = pl.multiple_of(start_blk * tq, tq)
        rows = pl.ds(pl.multiple_of(qi * tq, tq), tq)
        kb = k_ref[pl.ds(start, band_blocks * tq), :]
        vb = v_ref[pl.ds(start, band_blocks * tq), :]
        q = q_ref[rows, :]
        out = None
        for hh in range(HEADS_PER_STEP):
            qm = jnp.where(_head_lanes(width, hh), q, jnp.zeros_like(q))
            z = lax.dot_general(qm, kb, _CONTRACT_LAST, preferred_element_type=_F32)
            z = z + jnp.concatenate(
                [bias_ref[hh, bias_blk0 + t] for t in range(band_blocks)], axis=1)
            p = jnp.exp2(z - jnp.max(z, axis=-1, keepdims=True))
            denom = jnp.sum(p, axis=-1, keepdims=True)
            o = jnp.dot(p.astype(_BF16), vb, preferred_element_type=_F32) / denom
            if out is None:
                out = o
            else:
                lane = lax.broadcasted_iota(jnp.int32, (1, width), 1)
                out = jnp.where(lane >= hh * HEAD_DIM, o, out)
        o_ref[rows, :] = out.astype(_BF16)
        return c

    lax.fori_loop(0, nq, query_block, 0)


def _chunk_attention(qkv, rel_bias, batch, seq):
    t, d3 = qkv.shape
    d = d3 // 3
    tq = ATTN_ROWS
    width = HEADS_PER_STEP * HEAD_DIM
    left_blocks = LEFT_CHUNKS * CHUNK // tq
    master_blocks = 2 * left_blocks + 1
    q_spec, k_spec, v_spec, o_spec, hgs = _attn_specs(seq, d)
    bias_bytes = HEADS_PER_STEP * master_blocks * tq * tq * 4
    est = bias_bytes + 8 * seq * width * 2 + 8 * tq * 3 * tq * 4
    return pl.pallas_call(
        functools.partial(_chunk_attn_body, n_rel=rel_bias.shape[1], nq=seq // tq),
        grid=(hgs, batch),
        in_specs=[pl.BlockSpec(memory_space=pltpu.SMEM), q_spec, k_spec, v_spec],
        out_specs=o_spec,
        out_shape=jax.ShapeDtypeStruct((t, d), _BF16),
        scratch_shapes=[pltpu.VMEM((HEADS_PER_STEP, master_blocks, tq, tq), _F32)],
        compiler_params=pltpu.CompilerParams(
            dimension_semantics=("arbitrary", "arbitrary"),
            vmem_limit_bytes=_vmem_limit(est)),
        name="chunk_attention",
    )(rel_bias, qkv, qkv, qkv)


def _row_norm_lanes(x):
    xf = x.astype(_F32)
    sq = jnp.sum(xf * xf, axis=-1, keepdims=True)
    return jnp.broadcast_to(jnp.sqrt(sq), (x.shape[0], V7X_LANES))


def _stick_attn_body(q_ref, k_ref, v_ref, o_ref, qs_ref, acc_ref, carry_ref, reach_ref, *, nq):
    tq = ATTN_ROWS
    nh = HEADS_PER_STEP
    width = nh * HEAD_DIM

    r2 = lax.broadcasted_iota(jnp.int32, (2 * tq, tq), 0)
    c2 = lax.broadcasted_iota(jnp.int32, (2 * tq, tq), 1)
    neg_suffix = jnp.where(r2 % tq >= c2, -1.0, 0.0).astype(_BF16)

    def key_norm(j, m):
        kj = k_ref[pl.ds(pl.multiple_of(j * tq, tq), tq), :]
        return jnp.maximum(m, jnp.max(_row_norm_lanes(kj)))

    k_norm_max = lax.fori_loop(0, nq, key_norm, jnp.float32(0.0))

    def block(qi, j, diag):
        off = pl.multiple_of(j * tq, tq)
        kj = k_ref[pl.ds(off, tq), :]
        vj = v_ref[pl.ds(off, tq), :]
        z = lax.dot_general(qs_ref[...], kj, _CONTRACT_LAST, preferred_element_type=_F32)
        sp = jnp.maximum(z, 0.0) + jnp.log2(1.0 + jnp.exp2(-jnp.abs(z)))
        if diag:
            row = lax.broadcasted_iota(jnp.int32, z.shape, 0) % tq
            col = lax.broadcasted_iota(jnp.int32, z.shape, 1)
            causal = col < row
            sp = jnp.where(causal, sp, 0.0)
        hi = sp.astype(_BF16)
        lo = (sp - hi.astype(_F32)).astype(_BF16)
        cin = jnp.dot(jnp.concatenate([hi, lo], axis=1), neg_suffix, preferred_element_type=_F32)
        carry = carry_ref[...]
        logw = z + cin + jnp.concatenate([carry] * (tq // V7X_LANES), axis=1)
        a = jnp.exp2(logw)
        if diag:
            a = jnp.where(causal, a, 0.0)
        a = a.astype(_BF16)
        a_cat = jnp.concatenate([a[hh * tq:(hh + 1) * tq] for hh in range(nh)], axis=1)
        v_cat = jnp.concatenate(
            [jnp.where(_head_lanes(width, hh), vj, jnp.zeros_like(vj)) for hh in range(nh)], axis=0)
        acc_ref[...] += jnp.dot(a_cat, v_cat, preferred_element_type=_F32)
        carry = carry + jnp.broadcast_to(cin[:, 0:1], carry.shape)
        carry_ref[...] = carry
        return jnp.max(carry + reach_ref[...])

    def query_block(qi, c):
        rows = pl.ds(pl.multiple_of(qi * tq, tq), tq)
        q = q_ref[rows, :]
        for hh in range(nh):
            qs_ref[hh * tq:(hh + 1) * tq, :] = jnp.where(
                _head_lanes(width, hh), q, jnp.zeros_like(q))
        reach = _row_norm_lanes(q) * (k_norm_max * 1.02) + 1.0
        reach_ref[...] = jnp.concatenate([reach] * nh, axis=0)
        acc_ref[...] = jnp.zeros_like(acc_ref)
        carry_ref[...] = jnp.zeros_like(carry_ref)

        bound = block(qi, qi, True)

        def alive(state):
            step, bound = state
            return (step < qi) & (bound > DEAD_LOG2_WEIGHT)

        def earlier(state):
            step, _ = state
            return step + 1, block(qi, qi - 1 - step, False)

        lax.while_loop(alive, earlier, (jnp.int32(0), bound))
        o_ref[rows, :] = acc_ref[...].astype(_BF16)
        return c

    lax.fori_loop(0, nq, query_block, 0)


def _stick_attention(qkv, batch, seq):
    t, d3 = qkv.shape
    d = d3 // 3
    tq = ATTN_ROWS
    nh = HEADS_PER_STEP
    width = nh * HEAD_DIM
    q_spec, k_spec, v_spec, o_spec, hgs = _attn_specs(seq, d)
    est = (8 * seq * width * 2 + nh * tq * width * 2 + tq * width * 4
           + 2 * nh * tq * V7X_LANES * 4 + 10 * nh * tq * tq * 4)
    return pl.pallas_call(
        functools.partial(_stick_attn_body, nq=seq // tq),
        grid=(hgs, batch),
        in_specs=[q_spec, k_spec, v_spec],
        out_specs=o_spec,
        out_shape=jax.ShapeDtypeStruct((t, d), _BF16),
        scratch_shapes=[
            pltpu.VMEM((nh * tq, width), _BF16),
            pltpu.VMEM((tq, width), _F32),
            pltpu.VMEM((nh * tq, V7X_LANES), _F32),
            pltpu.VMEM((nh * tq, V7X_LANES), _F32),
        ],
        compiler_params=pltpu.CompilerParams(
            dimension_semantics=("arbitrary", "arbitrary"),
            vmem_limit_bytes=_vmem_limit(est)),
        name="stick_attention",
    )(qkv, qkv, qkv)


def _post_body(o_ref, x_ref, wo_ref, g_post_mix_ref, g_pre_ffn_ref, wg_ref, wu_ref, wd_ref,
               g_post_ffn_ref, out_ref):
    m = jnp.dot(o_ref[...], wo_ref[...], preferred_element_type=_F32)
    x1 = x_ref[...] + _rms_norm(m, g_post_mix_ref[...])
    h = _rms_norm(x1, g_pre_ffn_ref[...]).astype(_BF16)
    gate = jnp.dot(h, wg_ref[...], preferred_element_type=_F32)
    up = jnp.dot(h, wu_ref[...], preferred_element_type=_F32)
    act = (gate * jax.nn.sigmoid(gate) * up).astype(_BF16)
    y = jnp.dot(act, wd_ref[...], preferred_element_type=_F32)
    out_ref[...] = x1 + _rms_norm(y, g_post_ffn_ref[...])


def _post_mix_ffn(o, x2, wo, g_post_mix, g_pre_ffn, wg, wu, wd, g_post_ffn):
    t, d = x2.shape
    f = wg.shape[1]
    tm = TOKEN_ROWS
    est = (d * d + 3 * d * f) * 2 + 2 * tm * d * (2 + 4 + 4) + tm * (3 * d * 4 + 2 * f * 4 + f * 2 + d * 2)
    row = lambda i: (i, 0)
    return pl.pallas_call(
        _post_body,
        grid=(t // tm,),
        in_specs=[
            pl.BlockSpec((tm, d), row),
            pl.BlockSpec((tm, d), row),
            _resident((d, d)),
            _resident((1, d)),
            _resident((1, d)),
            _resident((d, f)),
            _resident((d, f)),
            _resident((f, d)),
            _resident((1, d)),
        ],
        out_specs=pl.BlockSpec((tm, d), row),
        out_shape=jax.ShapeDtypeStruct((t, d), _F32),
        compiler_params=pltpu.CompilerParams(
            dimension_semantics=("arbitrary",), vmem_limit_bytes=_vmem_limit(est)),
        name="post_mix_ffn",
    )(o, x2, wo, g_post_mix, g_pre_ffn, wg, wu, wd, g_post_ffn)


def kernel(x, g_pre_mix, g_post_mix, w_qkv, w_o, rel_bias, g_pre_ffn, g_post_ffn, w_gate, w_up, w_down):
    batch, seq, d = x.shape
    depth = w_qkv.shape[0]
    assert d % (HEADS_PER_STEP * HEAD_DIM) == 0 and seq % ATTN_ROWS == 0
    assert (LEFT_CHUNKS * CHUNK) % ATTN_ROWS == 0 and seq >= LEFT_CHUNKS * CHUNK + ATTN_ROWS
    assert (batch * seq) % TOKEN_ROWS == 0

    q_scale = LOG2E / math.sqrt(HEAD_DIM)
    col_scale = jnp.concatenate(
        [jnp.full((1, d), q_scale, _F32), jnp.ones((1, 2 * d), _F32)], axis=1)
    row = lambda g: g.reshape(1, d).astype(_F32)

    x2 = x.reshape(batch * seq, d)
    for i in range(depth):
        qkv = _qkv_proj(x2, row(g_pre_mix[i]), w_qkv[i].astype(_BF16), col_scale)
        if i % 2 == 0:
            o = _chunk_attention(qkv, rel_bias[i // 2].astype(_F32), batch, seq)
        else:
            o = _stick_attention(qkv, batch, seq)
        x2 = _post_mix_ffn(
            o, x2, w_o[i].astype(_BF16), row(g_post_mix[i]), row(g_pre_ffn[i]),
            w_gate[i].astype(_BF16), w_up[i].astype(_BF16), w_down[i].astype(_BF16),
            row(g_post_ffn[i]))
    return x2.reshape(batch, seq, d)
```

```python
import functools
import math

import jax
import jax.numpy as jnp
from jax import lax
from jax.experimental import pallas as pl
from jax.experimental.pallas import tpu as pltpu

HEAD_DIM = 64
CHUNK = 64
LEFT_CHUNKS = 8
REL_CLIP = 128
RMS_EPS = 1e-6
LOG2E = math.log2(math.e)
MASKED_LOGIT = -1e30
DEAD_LOG2_WEIGHT = -140.0

V7X_LANES = 128
V7X_MXU_DIM = 256
V7X_VMEM_BYTES = 64 * 1024 * 1024

HEADS_PER_STEP = V7X_MXU_DIM // HEAD_DIM
ATTN_ROWS = V7X_MXU_DIM
STICK_TOP_ROWS = ATTN_ROWS // 2
TOKEN_ROWS = 512

_F32 = jnp.float32
_BF16 = jnp.bfloat16
_CONTRACT_LAST = (((1,), (1,)), ((), ()))


def _vmem_limit(estimate_bytes):
    return int(min(estimate_bytes * 3 // 2, V7X_VMEM_BYTES * 15 // 16))


def _rms_norm(x, g):
    ms = jnp.mean(x * x, axis=-1, keepdims=True)
    return x * lax.rsqrt(ms + RMS_EPS) * g


def _resident(shape):
    return pl.BlockSpec(shape, lambda *_: (0,) * len(shape), pipeline_mode=pl.Buffered(1))


def _qkv_body(x_ref, g_ref, w_ref, s_ref, o_ref):
    h = _rms_norm(x_ref[...], g_ref[...]).astype(_BF16)
    y = jnp.dot(h, w_ref[...], preferred_element_type=_F32)
    o_ref[...] = (y * s_ref[...]).astype(_BF16)


def _qkv_proj(x2, g, w, col_scale):
    t, d = x2.shape
    n = w.shape[1]
    tm = TOKEN_ROWS
    est = 2 * tm * d * 4 + d * n * 2 + 2 * tm * n * 2 + tm * n * 4 + tm * d * 6
    return pl.pallas_call(
        _qkv_body,
        grid=(t // tm,),
        in_specs=[
            pl.BlockSpec((tm, d), lambda i: (i, 0)),
            _resident((1, d)),
            _resident((d, n)),
            _resident((1, n)),
        ],
        out_specs=pl.BlockSpec((tm, n), lambda i: (i, 0)),
        out_shape=jax.ShapeDtypeStruct((t, n), _BF16),
        compiler_params=pltpu.CompilerParams(
            dimension_semantics=("arbitrary",), vmem_limit_bytes=_vmem_limit(est)),
        name="qkv_proj",
    )(x2, g, w, col_scale)


def _head_lanes(width, hh):
    lane = lax.broadcasted_iota(jnp.int32, (1, width), 1)
    return (lane >= hh * HEAD_DIM) & (lane < (hh + 1) * HEAD_DIM)


def _attn_specs(seq, d):
    width = HEADS_PER_STEP * HEAD_DIM
    col_blocks = d // width
    q_spec = pl.BlockSpec((seq, width), lambda hg, b: (b, hg))
    k_spec = pl.BlockSpec((seq, width), lambda hg, b: (b, col_blocks + hg))
    v_spec = pl.BlockSpec((seq, width), lambda hg, b: (b, 2 * col_blocks + hg))
    o_spec = pl.BlockSpec((seq, width), lambda hg, b: (b, hg))
    return q_spec, k_spec, v_spec, o_spec, col_blocks


def _chunk_attn_body(rel_ref, q_ref, k_ref, v_ref, o_ref, bias_ref, *, n_rel, nq):
    hg, b = pl.program_id(0), pl.program_id(1)
    tq = ATTN_ROWS
    nh = HEADS_PER_STEP
    width = HEADS_PER_STEP * HEAD_DIM
    left = LEFT_CHUNKS * CHUNK
    left_blocks = left // tq
    band_blocks = left_blocks + 1
    master_blocks = band_blocks + left_blocks
    master = master_blocks * tq
    roll_w = master + tq

    @pl.when(b == 0)
    def _build_bias():
        wpos = lax.broadcasted_iota(jnp.int32, (8, roll_w), 1)
        ridx = jnp.clip(left + tq - wpos, -(CHUNK - 1), REL_CLIP) + (CHUNK - 1)
        i = lax.broadcasted_iota(jnp.int32, (tq, master), 0)
        u = lax.broadcasted_iota(jnp.int32, (tq, master), 1)
        d = left + i - u
        i_in_chunk = i % CHUNK
        visible = (d >= i_in_chunk - (CHUNK - 1)) & (d <= i_in_chunk + left)
        for hh in range(HEADS_PER_STEP):
            h = hg * HEADS_PER_STEP + hh

            def pick(r, base, h=h):
                return jnp.where(ridx == r, rel_ref[h, r], base)

            base = lax.fori_loop(0, n_rel, pick, jnp.zeros((8, roll_w), _F32))
            rows = jnp.broadcast_to(base[0:1, :] * LOG2E, (tq, roll_w))
            toeplitz = pltpu.roll(rows, master, 1, stride=1, stride_axis=0)
            tile = jnp.where(visible, toeplitz[:, :master], MASKED_LOGIT)
            for ub in range(master_blocks):
                bias_ref[hh, ub] = tile[:, ub * tq:(ub + 1) * tq]

    def query_block(qi, c):
        start_blk = jnp.maximum(qi - left_blocks, 0)
        bias_blk0 = left_blocks - (qi - start_blk)
        start = pl.multiple_of(start_blk * tq, tq)
        rows = pl.ds(pl.multiple_of(qi * tq, tq), tq)
        kb = k_ref[pl.ds(start, band_blocks * tq), :]
        vb = v_ref[pl.ds(start, band_blocks * tq), :]
        q = q_ref[rows, :]
        lane = lax.broadcasted_iota(jnp.int32, (1, width), 1)
        out = None
        for hh in range(nh):
            qm = jnp.where(_head_lanes(width, hh), q, jnp.zeros_like(q))
            z = lax.dot_general(qm, kb, _CONTRACT_LAST, preferred_element_type=_F32)
            z = z + jnp.concatenate(
                [bias_ref[hh, bias_blk0 + t] for t in range(band_blocks)], axis=1)
            p = jnp.exp2(z - jnp.max(z, axis=-1, keepdims=True))
            den = jnp.sum(p, axis=-1, keepdims=True)
            o = jnp.dot(p.astype(_BF16), vb, preferred_element_type=_F32) / den
            out = o if out is None else jnp.where(lane >= hh * HEAD_DIM, o, out)
        o_ref[rows, :] = out.astype(_BF16)
        return c

    lax.fori_loop(0, nq, query_block, 0, unroll=2)


def _chunk_attention(qkv, rel_bias, batch, seq):
    t, d3 = qkv.shape
    d = d3 // 3
    tq = ATTN_ROWS
    width = HEADS_PER_STEP * HEAD_DIM
    left_blocks = LEFT_CHUNKS * CHUNK // tq
    master_blocks = 2 * left_blocks + 1
    q_spec, k_spec, v_spec, o_spec, hgs = _attn_specs(seq, d)
    band = (left_blocks + 1) * tq
    bias_bytes = HEADS_PER_STEP * master_blocks * tq * tq * 4
    est = bias_bytes + 8 * seq * width * 2 + 2 * HEADS_PER_STEP * tq * band * 10
    return pl.pallas_call(
        functools.partial(_chunk_attn_body, n_rel=rel_bias.shape[1], nq=seq // tq),
        grid=(hgs, batch),
        in_specs=[pl.BlockSpec(memory_space=pltpu.SMEM), q_spec, k_spec, v_spec],
        out_specs=o_spec,
        out_shape=jax.ShapeDtypeStruct((t, d), _BF16),
        scratch_shapes=[pltpu.VMEM((HEADS_PER_STEP, master_blocks, tq, tq), _F32)],
        compiler_params=pltpu.CompilerParams(
            dimension_semantics=("arbitrary", "arbitrary"),
            vmem_limit_bytes=_vmem_limit(est)),
        name="chunk_attention",
    )(rel_bias, qkv, qkv, qkv)


def _row_norm_lanes(x):
    xf = x.astype(_F32)
    sq = jnp.sum(xf * xf, axis=-1, keepdims=True)
    return jnp.broadcast_to(jnp.sqrt(sq), (x.shape[0], V7X_LANES))


def _stick_attn_body(q_ref, k_ref, v_ref, o_ref, qs_ref, acc_ref, carry_ref, reach_ref, *, nq):
    tq = ATTN_ROWS
    nh = HEADS_PER_STEP
    width = nh * HEAD_DIM
    top = STICK_TOP_ROWS

    r2 = lax.broadcasted_iota(jnp.int32, (tq, tq), 0)
    c2 = lax.broadcasted_iota(jnp.int32, (tq, tq), 1)
    neg_suffix = jnp.where(r2 >= c2, -1.0, 0.0).astype(_BF16)

    def key_norm(j, m):
        kj = k_ref[pl.ds(pl.multiple_of(j * tq, tq), tq), :]
        return jnp.maximum(m, jnp.max(_row_norm_lanes(kj)))

    k_norm_max = lax.fori_loop(0, nq, key_norm, jnp.float32(0.0))

    def stacked(ref, n):
        if n == tq:
            return ref[...]
        return jnp.concatenate([ref[hh * tq:hh * tq + n] for hh in range(nh)], axis=0)

    def key_block(q_stack, carry, j, diag):
        n = q_stack.shape[0] // nh
        off = pl.multiple_of(j * tq, tq)
        kj = k_ref[pl.ds(off, tq), :]
        vj = v_ref[pl.ds(off, tq), :]
        z = lax.dot_general(q_stack, kj, _CONTRACT_LAST, preferred_element_type=_F32)
        neg_abs = pltpu.bitcast(pltpu.bitcast(z, jnp.uint32) | jnp.uint32(0x80000000), _F32)
        sp = jnp.maximum(z, 0.0) + jnp.log2(1.0 + jnp.exp2(neg_abs))
        if diag:
            row = lax.broadcasted_iota(jnp.int32, z.shape, 0) % n
            col = lax.broadcasted_iota(jnp.int32, z.shape, 1)
            causal = col < row
            sp = jnp.where(causal, sp, 0.0)
        cin = jnp.dot(sp.astype(_BF16), neg_suffix, preferred_element_type=_F32)
        logw = z + cin
        if carry is not None:
            logw = logw + jnp.concatenate([carry] * (tq // V7X_LANES), axis=1)
        a = jnp.exp2(logw)
        if diag:
            a = jnp.where(causal, a, 0.0)
        a = a.astype(_BF16)
        a_cat = jnp.concatenate([a[hh * n:(hh + 1) * n] for hh in range(nh)], axis=1)
        v_cat = jnp.concatenate(
            [jnp.where(_head_lanes(width, hh), vj, jnp.zeros_like(vj)) for hh in range(nh)], axis=0)
        pv = jnp.dot(a_cat, v_cat, preferred_element_type=_F32)
        total = jnp.broadcast_to(cin[:, 0:1], (nh * n, V7X_LANES))
        return pv, (total if carry is None else carry + total)

    def query_block(qi, c):
        rows = pl.ds(pl.multiple_of(qi * tq, tq), tq)
        q = q_ref[rows, :]
        for hh in range(nh):
            qs_ref[hh * tq:(hh + 1) * tq, :] = jnp.where(
                _head_lanes(width, hh), q, jnp.zeros_like(q))
        reach = _row_norm_lanes(q) * (k_norm_max * 1.02) + 1.0
        reach_ref[...] = jnp.concatenate([reach] * nh, axis=0)

        @pl.when(qi == 0)
        def _first():
            pv, _ = key_block(qs_ref[...], None, qi, True)
            o_ref[rows, :] = pv.astype(_BF16)

        @pl.when(qi > 0)
        def _rest():
            pv_d, carry = key_block(qs_ref[...], None, qi, True)
            pv_n, carry = key_block(qs_ref[...], carry, qi - 1, False)
            acc_ref[...] = pv_d + pv_n
            carry_ref[...] = carry
            slack = carry + reach_ref[...]
            row = lax.broadcasted_iota(jnp.int32, slack.shape, 0) % tq
            slack_top = jnp.max(jnp.where(row < top, slack, -jnp.inf))
            slack_low = jnp.max(jnp.where(row >= top, slack, -jnp.inf))
            low_rows_dead = slack_low <= DEAD_LOG2_WEIGHT

            def top_alive(state):
                step, slack = state
                return (step < qi) & (slack > DEAD_LOG2_WEIGHT) & low_rows_dead

            def top_step(state):
                step, _ = state
                pv, carry = key_block(stacked(qs_ref, top), stacked(carry_ref, top),
                                      qi - 1 - step, False)
                acc_ref[0:top, :] += pv
                for hh in range(nh):
                    carry_ref[hh * tq:hh * tq + top, :] = carry[hh * top:(hh + 1) * top]
                return step + 1, jnp.max(carry + stacked(reach_ref, top))

            lax.while_loop(top_alive, top_step, (jnp.int32(1), slack_top))

            def all_alive(state):
                step, slack = state
                return (step < qi) & (slack > DEAD_LOG2_WEIGHT) & jnp.logical_not(low_rows_dead)

            def all_step(state):
                step, _ = state
                pv, carry = key_block(qs_ref[...], carry_ref[...], qi - 1 - step, False)
                acc_ref[...] += pv
                carry_ref[...] = carry
                return step + 1, jnp.max(carry + reach_ref[...])

            lax.while_loop(all_alive, all_step, (jnp.int32(1), jnp.maximum(slack_top, slack_low)))
            o_ref[rows, :] = acc_ref[...].astype(_BF16)

        return c

    lax.fori_loop(0, nq, query_block, 0)


def _stick_attention(qkv, batch, seq):
    t, d3 = qkv.shape
    d = d3 // 3
    tq = ATTN_ROWS
    nh = HEADS_PER_STEP
    width = nh * HEAD_DIM
    q_spec, k_spec, v_spec, o_spec, hgs = _attn_specs(seq, d)
    est = (8 * seq * width * 2 + nh * tq * width * 2 + tq * width * 4
           + 2 * nh * tq * V7X_LANES * 4 + 10 * nh * tq * tq * 4)
    return pl.pallas_call(
        functools.partial(_stick_attn_body, nq=seq // tq),
        grid=(hgs, batch),
        in_specs=[q_spec, k_spec, v_spec],
        out_specs=o_spec,
        out_shape=jax.ShapeDtypeStruct((t, d), _BF16),
        scratch_shapes=[
            pltpu.VMEM((nh * tq, width), _BF16),
            pltpu.VMEM((tq, width), _F32),
            pltpu.VMEM((nh * tq, V7X_LANES), _F32),
            pltpu.VMEM((nh * tq, V7X_LANES), _F32),
        ],
        compiler_params=pltpu.CompilerParams(
            dimension_semantics=("arbitrary", "arbitrary"),
            vmem_limit_bytes=_vmem_limit(est)),
        name="stick_attention",
    )(qkv, qkv, qkv)


def _post_body(o_ref, x_ref, wo_ref, g_post_mix_ref, g_pre_ffn_ref, wg_ref, wu_ref, wd_ref,
               g_post_ffn_ref, out_ref):
    m = jnp.dot(o_ref[...], wo_ref[...], preferred_element_type=_F32)
    x1 = x_ref[...] + _rms_norm(m, g_post_mix_ref[...])
    h = _rms_norm(x1, g_pre_ffn_ref[...]).astype(_BF16)
    gate = jnp.dot(h, wg_ref[...], preferred_element_type=_F32)
    up = jnp.dot(h, wu_ref[...], preferred_element_type=_F32)
    act = (gate * jax.nn.sigmoid(gate) * up).astype(_BF16)
    y = jnp.dot(act, wd_ref[...], preferred_element_type=_F32)
    out_ref[...] = x1 + _rms_norm(y, g_post_ffn_ref[...])


def _post_mix_ffn(o, x2, wo, g_post_mix, g_pre_ffn, wg, wu, wd, g_post_ffn):
    t, d = x2.shape
    f = wg.shape[1]
    tm = TOKEN_ROWS
    est = (d * d + 3 * d * f) * 2 + 2 * tm * d * (2 + 4 + 4) + tm * (3 * d * 4 + 2 * f * 4 + f * 2 + d * 2)
    row = lambda i: (i, 0)
    return pl.pallas_call(
        _post_body,
        grid=(t // tm,),
        in_specs=[
            pl.BlockSpec((tm, d), row),
            pl.BlockSpec((tm, d), row),
            _resident((d, d)),
            _resident((1, d)),
            _resident((1, d)),
            _resident((d, f)),
            _resident((d, f)),
            _resident((f, d)),
            _resident((1, d)),
        ],
        out_specs=pl.BlockSpec((tm, d), row),
        out_shape=jax.ShapeDtypeStruct((t, d), _F32),
        compiler_params=pltpu.CompilerParams(
            dimension_semantics=("arbitrary",), vmem_limit_bytes=_vmem_limit(est)),
        name="post_mix_ffn",
    )(o, x2, wo, g_post_mix, g_pre_ffn, wg, wu, wd, g_post_ffn)


def kernel(x, g_pre_mix, g_post_mix, w_qkv, w_o, rel_bias, g_pre_ffn, g_post_ffn, w_gate, w_up, w_down):
    batch, seq, d = x.shape
    depth = w_qkv.shape[0]
    assert d % (HEADS_PER_STEP * HEAD_DIM) == 0 and seq % ATTN_ROWS == 0
    assert (LEFT_CHUNKS * CHUNK) % ATTN_ROWS == 0 and seq >= LEFT_CHUNKS * CHUNK + ATTN_ROWS
    assert (batch * seq) % TOKEN_ROWS == 0

    q_scale = LOG2E / math.sqrt(HEAD_DIM)
    col_scale = jnp.concatenate(
        [jnp.full((1, d), q_scale, _F32), jnp.ones((1, 2 * d), _F32)], axis=1)
    row = lambda g: g.reshape(1, d).astype(_F32)

    x2 = x.reshape(batch * seq, d)
    for i in range(depth):
        qkv = _qkv_proj(x2, row(g_pre_mix[i]), w_qkv[i].astype(_BF16), col_scale)
        if i % 2 == 0:
            o = _chunk_attention(qkv, rel_bias[i // 2].astype(_F32), batch, seq)
        else:
            o = _stick_attention(qkv, batch, seq)
        x2 = _post_mix_ffn(
            o, x2, w_o[i].astype(_BF16), row(g_post_mix[i]), row(g_pre_ffn[i]),
            w_gate[i].astype(_BF16), w_up[i].astype(_BF16), w_down[i].astype(_BF16),
            row(g_post_ffn[i]))
    return x2.reshape(batch, seq, d)
```

```python
import functools
import math

import jax
import jax.numpy as jnp
from jax import lax
from jax.experimental import pallas as pl
from jax.experimental.pallas import tpu as pltpu

HEAD_DIM = 64
CHUNK = 64
LEFT_CHUNKS = 8
REL_CLIP = 128
RMS_EPS = 1e-6
LOG2E = math.log2(math.e)
MASKED_LOGIT = -1e30
DEAD_LOG2_WEIGHT = -140.0

V7X_LANES = 128
V7X_MXU_DIM = 256
V7X_VMEM_BYTES = 64 * 1024 * 1024

HEADS_PER_STEP = V7X_MXU_DIM // HEAD_DIM
ATTN_ROWS = V7X_MXU_DIM
STICK_TOP_ROWS = ATTN_ROWS // 2
TOKEN_ROWS = 512

_F32 = jnp.float32
_BF16 = jnp.bfloat16
_CONTRACT_LAST = (((1,), (1,)), ((), ()))


def _vmem_limit(estimate_bytes):
    return int(min(estimate_bytes * 3 // 2, V7X_VMEM_BYTES * 15 // 16))


def _rms_norm(x, g):
    ms = jnp.mean(x * x, axis=-1, keepdims=True)
    return x * lax.rsqrt(ms + RMS_EPS) * g


def _resident(shape):
    return pl.BlockSpec(shape, lambda *_: (0,) * len(shape), pipeline_mode=pl.Buffered(1))


def _qkv_body(x_ref, g_ref, w_ref, s_ref, o_ref):
    h = _rms_norm(x_ref[...], g_ref[...]).astype(_BF16)
    y = jnp.dot(h, w_ref[...], preferred_element_type=_F32)
    o_ref[...] = (y * s_ref[...]).astype(_BF16)


def _qkv_proj(x2, g, w, col_scale):
    t, d = x2.shape
    n = w.shape[1]
    tm = TOKEN_ROWS
    est = 2 * tm * d * 4 + d * n * 2 + 2 * tm * n * 2 + tm * n * 4 + tm * d * 6
    return pl.pallas_call(
        _qkv_body,
        grid=(t // tm,),
        in_specs=[
            pl.BlockSpec((tm, d), lambda i: (i, 0)),
            _resident((1, d)),
            _resident((d, n)),
            _resident((1, n)),
        ],
        out_specs=pl.BlockSpec((tm, n), lambda i: (i, 0)),
        out_shape=jax.ShapeDtypeStruct((t, n), _BF16),
        compiler_params=pltpu.CompilerParams(
            dimension_semantics=("arbitrary",), vmem_limit_bytes=_vmem_limit(est)),
        name="qkv_proj",
    )(x2, g, w, col_scale)


def _head_lanes(width, hh):
    lane = lax.broadcasted_iota(jnp.int32, (1, width), 1)
    return (lane >= hh * HEAD_DIM) & (lane < (hh + 1) * HEAD_DIM)


def _attn_specs(seq, d):
    width = HEADS_PER_STEP * HEAD_DIM
    col_blocks = d // width
    q_spec = pl.BlockSpec((seq, width), lambda hg, b: (b, hg))
    k_spec = pl.BlockSpec((seq, width), lambda hg, b: (b, col_blocks + hg))
    v_spec = pl.BlockSpec((seq, width), lambda hg, b: (b, 2 * col_blocks + hg))
    o_spec = pl.BlockSpec((seq, width), lambda hg, b: (b, hg))
    return q_spec, k_spec, v_spec, o_spec, col_blocks


def _chunk_attn_body(rel_ref, q_ref, k_ref, v_ref, o_ref, bias_ref, *, n_rel, nq):
    hg, b = pl.program_id(0), pl.program_id(1)
    tq = ATTN_ROWS
    nh = HEADS_PER_STEP
    width = HEADS_PER_STEP * HEAD_DIM
    left = LEFT_CHUNKS * CHUNK
    left_blocks = left // tq
    band_blocks = left_blocks + 1
    master_blocks = band_blocks + left_blocks
    master = master_blocks * tq
    roll_w = master + tq

    @pl.when(b == 0)
    def _build_bias():
        wpos = lax.broadcasted_iota(jnp.int32, (8, roll_w), 1)
        ridx = jnp.clip(left + tq - wpos, -(CHUNK - 1), REL_CLIP) + (CHUNK - 1)
        i = lax.broadcasted_iota(jnp.int32, (tq, master), 0)
        u = lax.broadcasted_iota(jnp.int32, (tq, master), 1)
        d = left + i - u
        i_in_chunk = i % CHUNK
        visible = (d >= i_in_chunk - (CHUNK - 1)) & (d <= i_in_chunk + left)
        for hh in range(HEADS_PER_STEP):
            h = hg * HEADS_PER_STEP + hh

            def pick(r, base, h=h):
                return jnp.where(ridx == r, rel_ref[h, r], base)

            base = lax.fori_loop(0, n_rel, pick, jnp.zeros((8, roll_w), _F32))
            rows = jnp.broadcast_to(base[0:1, :] * LOG2E, (tq, roll_w))
            toeplitz = pltpu.roll(rows, master, 1, stride=1, stride_axis=0)
            tile = jnp.where(visible, toeplitz[:, :master], MASKED_LOGIT)
            for ub in range(master_blocks):
                bias_ref[hh, ub] = tile[:, ub * tq:(ub + 1) * tq]

    def query_block(qi, c):
        start_blk = jnp.maximum(qi - left_blocks, 0)
        bias_blk0 = left_blocks - (qi - start_blk)
        start = pl.multiple_of(start_blk * tq, tq)
        rows = pl.ds(pl.multiple_of(qi * tq, tq), tq)
        kb = k_ref[pl.ds(start, band_blocks * tq), :]
        vb = v_ref[pl.ds(start, band_blocks * tq), :]
        q = q_ref[rows, :]
        lane = lax.broadcasted_iota(jnp.int32, (1, width), 1)
        out = None
        for hh in range(nh):
            qm = jnp.where(_head_lanes(width, hh), q, jnp.zeros_like(q))
            z = lax.dot_general(qm, kb, _CONTRACT_LAST, preferred_element_type=_F32)
            z = z + jnp.concatenate(
                [bias_ref[hh, bias_blk0 + t] for t in range(band_blocks)], axis=1)
            p = jnp.exp2(z - jnp.max(z, axis=-1, keepdims=True))
            den = jnp.sum(p, axis=-1, keepdims=True)
            o = jnp.dot(p.astype(_BF16), vb, preferred_element_type=_F32) / den
            out = o if out is None else jnp.where(lane >= hh * HEAD_DIM, o, out)
        o_ref[rows, :] = out.astype(_BF16)
        return c

    lax.fori_loop(0, nq, query_block, 0, unroll=2)


def _chunk_attention(qkv, rel_bias, batch, seq):
    t, d3 = qkv.shape
    d = d3 // 3
    tq = ATTN_ROWS
    width = HEADS_PER_STEP * HEAD_DIM
    left_blocks = LEFT_CHUNKS * CHUNK // tq
    master_blocks = 2 * left_blocks + 1
    q_spec, k_spec, v_spec, o_spec, hgs = _attn_specs(seq, d)
    band = (left_blocks + 1) * tq
    bias_bytes = HEADS_PER_STEP * master_blocks * tq * tq * 4
    est = bias_bytes + 8 * seq * width * 2 + 2 * HEADS_PER_STEP * tq * band * 10
    return pl.pallas_call(
        functools.partial(_chunk_attn_body, n_rel=rel_bias.shape[1], nq=seq // tq),
        grid=(hgs, batch),
        in_specs=[pl.BlockSpec(memory_space=pltpu.SMEM), q_spec, k_spec, v_spec],
        out_specs=o_spec,
        out_shape=jax.ShapeDtypeStruct((t, d), _BF16),
        scratch_shapes=[pltpu.VMEM((HEADS_PER_STEP, master_blocks, tq, tq), _F32)],
        compiler_params=pltpu.CompilerParams(
            dimension_semantics=("arbitrary", "arbitrary"),
            vmem_limit_bytes=_vmem_limit(est)),
        name="chunk_attention",
    )(rel_bias, qkv, qkv, qkv)


def _row_norm_lanes(x):
    xf = x.astype(_F32)
    sq = jnp.sum(xf * xf, axis=-1, keepdims=True)
    return jnp.broadcast_to(jnp.sqrt(sq), (x.shape[0], V7X_LANES))


def _stick_attn_body(q_ref, k_ref, v_ref, o_ref, qs_ref, acc_ref, carry_ref, reach_ref, *, nq):
    tq = ATTN_ROWS
    nh = HEADS_PER_STEP
    width = nh * HEAD_DIM
    top = STICK_TOP_ROWS

    r2 = lax.broadcasted_iota(jnp.int32, (tq, tq), 0)
    c2 = lax.broadcasted_iota(jnp.int32, (tq, tq), 1)
    neg_suffix = jnp.where(r2 >= c2, -1.0, 0.0).astype(_BF16)

    def key_sq_norm(j, m):
        kj = k_ref[pl.ds(pl.multiple_of(j * tq, tq), tq), :].astype(_F32)
        sq = jnp.sum(kj * kj, axis=-1, keepdims=True)
        return jnp.maximum(m, jnp.broadcast_to(sq, m.shape))

    k_norm_max = jnp.sqrt(jnp.max(
        lax.fori_loop(0, nq, key_sq_norm, jnp.zeros((tq, V7X_LANES), _F32))))

    def stacked(ref, n):
        if n == tq:
            return ref[...]
        return jnp.concatenate([ref[hh * tq:hh * tq + n] for hh in range(nh)], axis=0)

    def key_block(q_stack, carry, j, diag):
        n = q_stack.shape[0] // nh
        off = pl.multiple_of(j * tq, tq)
        kj = k_ref[pl.ds(off, tq), :]
        vj = v_ref[pl.ds(off, tq), :]
        z = lax.dot_general(q_stack, kj, _CONTRACT_LAST, preferred_element_type=_F32)
        sp = jnp.maximum(z, 0.0) + jnp.log2(1.0 + jnp.exp2(-jnp.abs(z)))
        if diag:
            row = lax.broadcasted_iota(jnp.int32, z.shape, 0) % n
            col = lax.broadcasted_iota(jnp.int32, z.shape, 1)
            causal = col < row
            sp = jnp.where(causal, sp, 0.0)
        cin = jnp.dot(sp.astype(_BF16), neg_suffix, preferred_element_type=_F32)
        logw = z + cin
        if carry is not None:
            logw = logw + jnp.concatenate([carry] * (tq // V7X_LANES), axis=1)
        a = jnp.exp2(logw)
        if diag:
            a = jnp.where(causal, a, 0.0)
        a = a.astype(_BF16)
        a_cat = jnp.concatenate([a[hh * n:(hh + 1) * n] for hh in range(nh)], axis=1)
        v_cat = jnp.concatenate(
            [jnp.where(_head_lanes(width, hh), vj, jnp.zeros_like(vj)) for hh in range(nh)], axis=0)
        pv = jnp.dot(a_cat, v_cat, preferred_element_type=_F32)
        total = jnp.broadcast_to(cin[:, 0:1], (nh * n, V7X_LANES))
        return pv, (total if carry is None else carry + total)

    def query_block(qi, c):
        rows = pl.ds(pl.multiple_of(qi * tq, tq), tq)
        q = q_ref[rows, :]
        for hh in range(nh):
            qs_ref[hh * tq:(hh + 1) * tq, :] = jnp.where(
                _head_lanes(width, hh), q, jnp.zeros_like(q))
        reach = _row_norm_lanes(q) * (k_norm_max * 1.02) + 1.0
        reach_ref[...] = jnp.concatenate([reach] * nh, axis=0)

        @pl.when(qi == 0)
        def _first():
            pv, _ = key_block(qs_ref[...], None, qi, True)
            o_ref[rows, :] = pv.astype(_BF16)

        @pl.when(qi > 0)
        def _rest():
            pv_d, carry = key_block(qs_ref[...], None, qi, True)
            pv_n, carry = key_block(qs_ref[...], carry, qi - 1, False)
            acc_ref[...] = pv_d + pv_n
            carry_ref[...] = carry
            slack = carry + reach_ref[...]
            row = lax.broadcasted_iota(jnp.int32, slack.shape, 0) % tq
            slack_top = jnp.max(jnp.where(row < top, slack, -jnp.inf))
            slack_low = jnp.max(jnp.where(row >= top, slack, -jnp.inf))
            low_rows_dead = slack_low <= DEAD_LOG2_WEIGHT

            def top_alive(state):
                step, slack = state
                return (step < qi) & (slack > DEAD_LOG2_WEIGHT) & low_rows_dead

            def top_step(state):
                step, _ = state
                pv, carry = key_block(stacked(qs_ref, top), stacked(carry_ref, top),
                                      qi - 1 - step, False)
                acc_ref[0:top, :] += pv
                for hh in range(nh):
                    carry_ref[hh * tq:hh * tq + top, :] = carry[hh * top:(hh + 1) * top]
                return step + 1, jnp.max(carry + stacked(reach_ref, top))

            lax.while_loop(top_alive, top_step, (jnp.int32(1), slack_top))

            def all_alive(state):
                step, slack = state
                return (step < qi) & (slack > DEAD_LOG2_WEIGHT) & jnp.logical_not(low_rows_dead)

            def all_step(state):
                step, _ = state
                pv, carry = key_block(qs_ref[...], carry_ref[...], qi - 1 - step, False)
                acc_ref[...] += pv
                carry_ref[...] = carry
                return step + 1, jnp.max(carry + reach_ref[...])

            lax.while_loop(all_alive, all_step, (jnp.int32(1), jnp.maximum(slack_top, slack_low)))
            o_ref[rows, :] = acc_ref[...].astype(_BF16)

        return c

    lax.fori_loop(0, nq, query_block, 0)


def _stick_attention(qkv, batch, seq):
    t, d3 = qkv.shape
    d = d3 // 3
    tq = ATTN_ROWS
    nh = HEADS_PER_STEP
    width = nh * HEAD_DIM
    q_spec, k_spec, v_spec, o_spec, hgs = _attn_specs(seq, d)
    est = (8 * seq * width * 2 + nh * tq * width * 2 + tq * width * 4
           + 2 * nh * tq * V7X_LANES * 4 + 10 * nh * tq * tq * 4)
    return pl.pallas_call(
        functools.partial(_stick_attn_body, nq=seq // tq),
        grid=(hgs, batch),
        in_specs=[q_spec, k_spec, v_spec],
        out_specs=o_spec,
        out_shape=jax.ShapeDtypeStruct((t, d), _BF16),
        scratch_shapes=[
            pltpu.VMEM((nh * tq, width), _BF16),
            pltpu.VMEM((tq, width), _F32),
            pltpu.VMEM((nh * tq, V7X_LANES), _F32),
            pltpu.VMEM((nh * tq, V7X_LANES), _F32),
        ],
        compiler_params=pltpu.CompilerParams(
            dimension_semantics=("arbitrary", "arbitrary"),
            vmem_limit_bytes=_vmem_limit(est)),
        name="stick_attention",
    )(qkv, qkv, qkv)


def _post_body(o_ref, x_ref, wo_ref, g_post_mix_ref, g_pre_ffn_ref, wg_ref, wu_ref, wd_ref,
               g_post_ffn_ref, out_ref):
    m = jnp.dot(o_ref[...], wo_ref[...], preferred_element_type=_F32)
    x1 = x_ref[...] + _rms_norm(m, g_post_mix_ref[...])
    h = _rms_norm(x1, g_pre_ffn_ref[...]).astype(_BF16)
    gate = jnp.dot(h, wg_ref[...], preferred_element_type=_F32)
    up = jnp.dot(h, wu_ref[...], preferred_element_type=_F32)
    act = (gate * jax.nn.sigmoid(gate) * up).astype(_BF16)
    y = jnp.dot(act, wd_ref[...], preferred_element_type=_F32)
    out_ref[...] = x1 + _rms_norm(y, g_post_ffn_ref[...])


def _post_mix_ffn(o, x2, wo, g_post_mix, g_pre_ffn, wg, wu, wd, g_post_ffn):
    t, d = x2.shape
    f = wg.shape[1]
    tm = TOKEN_ROWS
    est = (d * d + 3 * d * f) * 2 + 2 * tm * d * (2 + 4 + 4) + tm * (3 * d * 4 + 2 * f * 4 + f * 2 + d * 2)
    row = lambda i: (i, 0)
    return pl.pallas_call(
        _post_body,
        grid=(t // tm,),
        in_specs=[
            pl.BlockSpec((tm, d), row),
            pl.BlockSpec((tm, d), row),
            _resident((d, d)),
            _resident((1, d)),
            _resident((1, d)),
            _resident((d, f)),
            _resident((d, f)),
            _resident((f, d)),
            _resident((1, d)),
        ],
        out_specs=pl.BlockSpec((tm, d), row),
        out_shape=jax.ShapeDtypeStruct((t, d), _F32),
        compiler_params=pltpu.CompilerParams(
            dimension_semantics=("arbitrary",), vmem_limit_bytes=_vmem_limit(est)),
        name="post_mix_ffn",
    )(o, x2, wo, g_post_mix, g_pre_ffn, wg, wu, wd, g_post_ffn)


def kernel(x, g_pre_mix, g_post_mix, w_qkv, w_o, rel_bias, g_pre_ffn, g_post_ffn, w_gate, w_up, w_down):
    batch, seq, d = x.shape
    depth = w_qkv.shape[0]
    assert d % (HEADS_PER_STEP * HEAD_DIM) == 0 and seq % ATTN_ROWS == 0
    assert (LEFT_CHUNKS * CHUNK) % ATTN_ROWS == 0 and seq >= LEFT_CHUNKS * CHUNK + ATTN_ROWS
    assert (batch * seq) % TOKEN_ROWS == 0

    q_scale = LOG2E / math.sqrt(HEAD_DIM)
    col_scale = jnp.concatenate(
        [jnp.full((1, d), q_scale, _F32), jnp.ones((1, 2 * d), _F32)], axis=1)
    row = lambda g: g.reshape(1, d).astype(_F32)

    x2 = x.reshape(batch * seq, d)
    for i in range(depth):
        qkv = _qkv_proj(x2, row(g_pre_mix[i]), w_qkv[i].astype(_BF16), col_scale)
        if i % 2 == 0:
            o = _chunk_attention(qkv, rel_bias[i // 2].astype(_F32), batch, seq)
        else:
            o = _stick_attention(qkv, batch, seq)
        x2 = _post_mix_ffn(
            o, x2, w_o[i].astype(_BF16), row(g_post_mix[i]), row(g_pre_ffn[i]),
            w_gate[i].astype(_BF16), w_up[i].astype(_BF16), w_down[i].astype(_BF16),
            row(g_post_ffn[i]))
    return x2.reshape(batch, seq, d)
```

```python
import functools
import math

import jax
import jax.numpy as jnp
from jax import lax
from jax.experimental import pallas as pl
from jax.experimental.pallas import tpu as pltpu

HEAD_DIM = 64
CHUNK = 64
LEFT_CHUNKS = 8
REL_CLIP = 128
RMS_EPS = 1e-6
LOG2E = math.log2(math.e)
MASKED_LOGIT = -1e30
DEAD_LOG2_WEIGHT = -140.0
SOFTPLUS_LINEAR_ABOVE = 100.0

V7X_LANES = 128
V7X_MXU_DIM = 256
V7X_VMEM_BYTES = 64 * 1024 * 1024

HEADS_PER_STEP = V7X_MXU_DIM // HEAD_DIM
ATTN_ROWS = V7X_MXU_DIM
STICK_TOP_ROWS = ATTN_ROWS // 2
TOKEN_ROWS = 512

_F32 = jnp.float32
_BF16 = jnp.bfloat16
_CONTRACT_LAST = (((1,), (1,)), ((), ()))


def _vmem_limit(estimate_bytes):
    return int(min(estimate_bytes * 3 // 2, V7X_VMEM_BYTES * 15 // 16))


def _rms_norm(x, g):
    ms = jnp.mean(x * x, axis=-1, keepdims=True)
    return x * lax.rsqrt(ms + RMS_EPS) * g


def _resident(shape):
    return pl.BlockSpec(shape, lambda *_: (0,) * len(shape), pipeline_mode=pl.Buffered(1))


def _qkv_body(x_ref, g_ref, w_ref, s_ref, o_ref):
    h = _rms_norm(x_ref[...], g_ref[...]).astype(_BF16)
    y = jnp.dot(h, w_ref[...], preferred_element_type=_F32)
    o_ref[...] = (y * s_ref[...]).astype(_BF16)


def _qkv_proj(x2, g, w, col_scale):
    t, d = x2.shape
    n = w.shape[1]
    tm = TOKEN_ROWS
    est = 2 * tm * d * 4 + d * n * 2 + 2 * tm * n * 2 + tm * n * 4 + tm * d * 6
    return pl.pallas_call(
        _qkv_body,
        grid=(t // tm,),
        in_specs=[
            pl.BlockSpec((tm, d), lambda i: (i, 0)),
            _resident((1, d)),
            _resident((d, n)),
            _resident((1, n)),
        ],
        out_specs=pl.BlockSpec((tm, n), lambda i: (i, 0)),
        out_shape=jax.ShapeDtypeStruct((t, n), _BF16),
        compiler_params=pltpu.CompilerParams(
            dimension_semantics=("arbitrary",), vmem_limit_bytes=_vmem_limit(est)),
        name="qkv_proj",
    )(x2, g, w, col_scale)


def _head_lanes(width, hh):
    lane = lax.broadcasted_iota(jnp.int32, (1, width), 1)
    return (lane >= hh * HEAD_DIM) & (lane < (hh + 1) * HEAD_DIM)


def _attn_specs(seq, d):
    width = HEADS_PER_STEP * HEAD_DIM
    col_blocks = d // width
    q_spec = pl.BlockSpec((seq, width), lambda hg, b: (b, hg))
    k_spec = pl.BlockSpec((seq, width), lambda hg, b: (b, col_blocks + hg))
    v_spec = pl.BlockSpec((seq, width), lambda hg, b: (b, 2 * col_blocks + hg))
    o_spec = pl.BlockSpec((seq, width), lambda hg, b: (b, hg))
    return q_spec, k_spec, v_spec, o_spec, col_blocks


def _chunk_attn_body(rel_ref, q_ref, k_ref, v_ref, o_ref, bias_ref, *, n_rel, nq):
    hg, b = pl.program_id(0), pl.program_id(1)
    tq = ATTN_ROWS
    nh = HEADS_PER_STEP
    width = HEADS_PER_STEP * HEAD_DIM
    left = LEFT_CHUNKS * CHUNK
    left_blocks = left // tq
    band_blocks = left_blocks + 1
    master_blocks = band_blocks + left_blocks
    master = master_blocks * tq
    roll_w = master + tq

    @pl.when(b == 0)
    def _build_bias():
        wpos = lax.broadcasted_iota(jnp.int32, (8, roll_w), 1)
        ridx = jnp.clip(left + tq - wpos, -(CHUNK - 1), REL_CLIP) + (CHUNK - 1)
        i = lax.broadcasted_iota(jnp.int32, (tq, master), 0)
        u = lax.broadcasted_iota(jnp.int32, (tq, master), 1)
        d = left + i - u
        i_in_chunk = i % CHUNK
        visible = (d >= i_in_chunk - (CHUNK - 1)) & (d <= i_in_chunk + left)
        for hh in range(HEADS_PER_STEP):
            h = hg * HEADS_PER_STEP + hh

            def pick(r, base, h=h):
                return jnp.where(ridx == r, rel_ref[h, r], base)

            base = lax.fori_loop(0, n_rel, pick, jnp.zeros((8, roll_w), _F32))
            rows = jnp.broadcast_to(base[0:1, :] * LOG2E, (tq, roll_w))
            toeplitz = pltpu.roll(rows, master, 1, stride=1, stride_axis=0)
            tile = jnp.where(visible, toeplitz[:, :master], MASKED_LOGIT)
            for ub in range(master_blocks):
                bias_ref[hh, ub] = tile[:, ub * tq:(ub + 1) * tq]

    lane = lax.broadcasted_iota(jnp.int32, (1, width), 1)
    for qi in range(nq):
        start_blk = max(qi - left_blocks, 0)
        n_blocks = qi - start_blk + 1
        bias_blk0 = left_blocks - (qi - start_blk)
        kb = k_ref[start_blk * tq:(qi + 1) * tq, :]
        vb = v_ref[start_blk * tq:(qi + 1) * tq, :]
        q = q_ref[qi * tq:(qi + 1) * tq, :]
        out = None
        for hh in range(nh):
            qm = jnp.where(_head_lanes(width, hh), q, jnp.zeros_like(q))
            z = lax.dot_general(qm, kb, _CONTRACT_LAST, preferred_element_type=_F32)
            z = z + jnp.concatenate(
                [bias_ref[hh, bias_blk0 + t] for t in range(n_blocks)], axis=1)
            p = jnp.exp2(z - jnp.max(z, axis=-1, keepdims=True))
            den = jnp.sum(p, axis=-1, keepdims=True)
            o = jnp.dot(p.astype(_BF16), vb, preferred_element_type=_F32) / den
            out = o if out is None else jnp.where(lane >= hh * HEAD_DIM, o, out)
        o_ref[qi * tq:(qi + 1) * tq, :] = out.astype(_BF16)


def _chunk_attention(qkv, rel_bias, batch, seq):
    t, d3 = qkv.shape
    d = d3 // 3
    tq = ATTN_ROWS
    width = HEADS_PER_STEP * HEAD_DIM
    left_blocks = LEFT_CHUNKS * CHUNK // tq
    master_blocks = 2 * left_blocks + 1
    q_spec, k_spec, v_spec, o_spec, hgs = _attn_specs(seq, d)
    band = (left_blocks + 1) * tq
    bias_bytes = HEADS_PER_STEP * master_blocks * tq * tq * 4
    est = bias_bytes + 8 * seq * width * 2 + 2 * HEADS_PER_STEP * tq * band * 10
    return pl.pallas_call(
        functools.partial(_chunk_attn_body, n_rel=rel_bias.shape[1], nq=seq // tq),
        grid=(hgs, batch),
        in_specs=[pl.BlockSpec(memory_space=pltpu.SMEM), q_spec, k_spec, v_spec],
        out_specs=o_spec,
        out_shape=jax.ShapeDtypeStruct((t, d), _BF16),
        scratch_shapes=[pltpu.VMEM((HEADS_PER_STEP, master_blocks, tq, tq), _F32)],
        compiler_params=pltpu.CompilerParams(
            dimension_semantics=("arbitrary", "arbitrary"),
            vmem_limit_bytes=_vmem_limit(est)),
        name="chunk_attention",
    )(rel_bias, qkv, qkv, qkv)


def _row_norm_lanes(x):
    xf = x.astype(_F32)
    sq = jnp.sum(xf * xf, axis=-1, keepdims=True)
    return jnp.broadcast_to(jnp.sqrt(sq), (x.shape[0], V7X_LANES))


def _stick_attn_body(q_ref, k_ref, v_ref, o_ref, qs_ref, acc_ref, carry_ref, reach_ref, *, nq):
    tq = ATTN_ROWS
    nh = HEADS_PER_STEP
    width = nh * HEAD_DIM
    top = STICK_TOP_ROWS

    r2 = lax.broadcasted_iota(jnp.int32, (tq, tq), 0)
    c2 = lax.broadcasted_iota(jnp.int32, (tq, tq), 1)
    neg_suffix = jnp.where(r2 >= c2, -1.0, 0.0).astype(_BF16)

    def key_sq_norm(j, m):
        kj = k_ref[pl.ds(pl.multiple_of(j * tq, tq), tq), :].astype(_F32)
        sq = jnp.sum(kj * kj, axis=-1, keepdims=True)
        return jnp.maximum(m, jnp.broadcast_to(sq, m.shape))

    k_norm_max = jnp.sqrt(jnp.max(
        lax.fori_loop(0, nq, key_sq_norm, jnp.zeros((tq, V7X_LANES), _F32))))

    def stacked(ref, n):
        if n == tq:
            return ref[...]
        return jnp.concatenate([ref[hh * tq:hh * tq + n] for hh in range(nh)], axis=0)

    def key_block(q_stack, carry, j, diag):
        n = q_stack.shape[0] // nh
        off = pl.multiple_of(j * tq, tq)
        kj = k_ref[pl.ds(off, tq), :]
        vj = v_ref[pl.ds(off, tq), :]
        z = lax.dot_general(q_stack, kj, _CONTRACT_LAST, preferred_element_type=_F32)
        sp = jnp.where(z > SOFTPLUS_LINEAR_ABOVE, z, jnp.log2(1.0 + jnp.exp2(z)))
        if diag:
            row = lax.broadcasted_iota(jnp.int32, z.shape, 0) % n
            col = lax.broadcasted_iota(jnp.int32, z.shape, 1)
            causal = col < row
            sp = jnp.where(causal, sp, 0.0)
        cin = jnp.dot(sp.astype(_BF16), neg_suffix, preferred_element_type=_F32)
        logw = z + cin
        if carry is not None:
            logw = logw + jnp.concatenate([carry] * (tq // V7X_LANES), axis=1)
        a = jnp.exp2(logw)
        if diag:
            a = jnp.where(causal, a, 0.0)
        a = a.astype(_BF16)
        a_cat = jnp.concatenate([a[hh * n:(hh + 1) * n] for hh in range(nh)], axis=1)
        v_cat = jnp.concatenate(
            [jnp.where(_head_lanes(width, hh), vj, jnp.zeros_like(vj)) for hh in range(nh)], axis=0)
        pv = jnp.dot(a_cat, v_cat, preferred_element_type=_F32)
        total = jnp.broadcast_to(cin[:, 0:1], (nh * n, V7X_LANES))
        return pv, (total if carry is None else carry + total)

    def query_block(qi, c):
        rows = pl.ds(pl.multiple_of(qi * tq, tq), tq)
        q = q_ref[rows, :]
        for hh in range(nh):
            qs_ref[hh * tq:(hh + 1) * tq, :] = jnp.where(
                _head_lanes(width, hh), q, jnp.zeros_like(q))
        reach = _row_norm_lanes(q) * (k_norm_max * 1.02) + 1.0
        reach_ref[...] = jnp.concatenate([reach] * nh, axis=0)

        @pl.when(qi == 0)
        def _first():
            pv, _ = key_block(qs_ref[...], None, qi, True)
            o_ref[rows, :] = pv.astype(_BF16)

        @pl.when(qi == 1)
        def _second():
            pv_d, carry = key_block(qs_ref[...], None, qi, True)
            pv_n, _ = key_block(qs_ref[...], carry, qi - 1, False)
            o_ref[rows, :] = (pv_d + pv_n).astype(_BF16)

        @pl.when(qi > 1)
        def _rest():
            pv_d, carry = key_block(qs_ref[...], None, qi, True)
            pv_n, carry = key_block(qs_ref[...], carry, qi - 1, False)
            carry_top = jnp.concatenate([carry[hh * tq:hh * tq + top] for hh in range(nh)], axis=0)
            pv_t, carry_top2 = key_block(stacked(qs_ref, top), carry_top, qi - 2, False)
            slack = carry + reach_ref[...]
            row = lax.broadcasted_iota(jnp.int32, slack.shape, 0) % tq
            slack_low = jnp.max(jnp.where(row >= top, slack, -jnp.inf))
            slack_all = jnp.max(slack)
            low_rows_dead = slack_low <= DEAD_LOG2_WEIGHT
            acc_ref[...] = pv_d + pv_n
            acc_ref[0:top, :] += jnp.where(low_rows_dead, pv_t, 0.0)
            carry_ref[...] = carry
            carry_top2 = jnp.where(low_rows_dead, carry_top2, carry_top)
            for hh in range(nh):
                carry_ref[hh * tq:hh * tq + top, :] = carry_top2[hh * top:(hh + 1) * top]
            slack_top = jnp.max(carry_top2 + stacked(reach_ref, top))

            def top_alive(state):
                step, slack = state
                return (step < qi) & (slack > DEAD_LOG2_WEIGHT) & low_rows_dead

            def top_step(state):
                step, _ = state
                pv, carry = key_block(stacked(qs_ref, top), stacked(carry_ref, top),
                                      qi - 1 - step, False)
                acc_ref[0:top, :] += pv
                for hh in range(nh):
                    carry_ref[hh * tq:hh * tq + top, :] = carry[hh * top:(hh + 1) * top]
                return step + 1, jnp.max(carry + stacked(reach_ref, top))

            lax.while_loop(top_alive, top_step, (jnp.int32(2), slack_top))

            def all_alive(state):
                step, slack = state
                return (step < qi) & (slack > DEAD_LOG2_WEIGHT) & jnp.logical_not(low_rows_dead)

            def all_step(state):
                step, _ = state
                pv, carry = key_block(qs_ref[...], carry_ref[...], qi - 1 - step, False)
                acc_ref[...] += pv
                carry_ref[...] = carry
                return step + 1, jnp.max(carry + reach_ref[...])

            lax.while_loop(all_alive, all_step, (jnp.int32(1), slack_all))
            o_ref[rows, :] = acc_ref[...].astype(_BF16)

        return c

    lax.fori_loop(0, nq, query_block, 0)


def _stick_attention(qkv, batch, seq):
    t, d3 = qkv.shape
    d = d3 // 3
    tq = ATTN_ROWS
    nh = HEADS_PER_STEP
    width = nh * HEAD_DIM
    q_spec, k_spec, v_spec, o_spec, hgs = _attn_specs(seq, d)
    est = (8 * seq * width * 2 + nh * tq * width * 2 + tq * width * 4
           + 2 * nh * tq * V7X_LANES * 4 + 10 * nh * tq * tq * 4)
    return pl.pallas_call(
        functools.partial(_stick_attn_body, nq=seq // tq),
        grid=(hgs, batch),
        in_specs=[q_spec, k_spec, v_spec],
        out_specs=o_spec,
        out_shape=jax.ShapeDtypeStruct((t, d), _BF16),
        scratch_shapes=[
            pltpu.VMEM((nh * tq, width), _BF16),
            pltpu.VMEM((tq, width), _F32),
            pltpu.VMEM((nh * tq, V7X_LANES), _F32),
            pltpu.VMEM((nh * tq, V7X_LANES), _F32),
        ],
        compiler_params=pltpu.CompilerParams(
            dimension_semantics=("arbitrary", "arbitrary"),
            vmem_limit_bytes=_vmem_limit(est)),
        name="stick_attention",
    )(qkv, qkv, qkv)


def _post_body(o_ref, x_ref, wo_ref, g_post_mix_ref, g_pre_ffn_ref, wg_ref, wu_ref, wd_ref,
               g_post_ffn_ref, out_ref):
    m = jnp.dot(o_ref[...], wo_ref[...], preferred_element_type=_F32)
    x1 = x_ref[...] + _rms_norm(m, g_post_mix_ref[...])
    h = _rms_norm(x1, g_pre_ffn_ref[...]).astype(_BF16)
    gate = jnp.dot(h, wg_ref[...], preferred_element_type=_F32)
    up = jnp.dot(h, wu_ref[...], preferred_element_type=_F32)
    act = (gate * jax.nn.sigmoid(gate) * up).astype(_BF16)
    y = jnp.dot(act, wd_ref[...], preferred_element_type=_F32)
    out_ref[...] = x1 + _rms_norm(y, g_post_ffn_ref[...])


def _post_mix_ffn(o, x2, wo, g_post_mix, g_pre_ffn, wg, wu, wd, g_post_ffn):
    t, d = x2.shape
    f = wg.shape[1]
    tm = TOKEN_ROWS
    est = (d * d + 3 * d * f) * 2 + 2 * tm * d * (2 + 4 + 4) + tm * (3 * d * 4 + 2 * f * 4 + f * 2 + d * 2)
    row = lambda i: (i, 0)
    return pl.pallas_call(
        _post_body,
        grid=(t // tm,),
        in_specs=[
            pl.BlockSpec((tm, d), row),
            pl.BlockSpec((tm, d), row),
            _resident((d, d)),
            _resident((1, d)),
            _resident((1, d)),
            _resident((d, f)),
            _resident((d, f)),
            _resident((f, d)),
            _resident((1, d)),
        ],
        out_specs=pl.BlockSpec((tm, d), row),
        out_shape=jax.ShapeDtypeStruct((t, d), _F32),
        compiler_params=pltpu.CompilerParams(
            dimension_semantics=("arbitrary",), vmem_limit_bytes=_vmem_limit(est)),
        name="post_mix_ffn",
    )(o, x2, wo, g_post_mix, g_pre_ffn, wg, wu, wd, g_post_ffn)


def kernel(x, g_pre_mix, g_post_mix, w_qkv, w_o, rel_bias, g_pre_ffn, g_post_ffn, w_gate, w_up, w_down):
    batch, seq, d = x.shape
    depth = w_qkv.shape[0]
    assert d % (HEADS_PER_STEP * HEAD_DIM) == 0 and seq % ATTN_ROWS == 0
    assert (LEFT_CHUNKS * CHUNK) % ATTN_ROWS == 0 and seq >= LEFT_CHUNKS * CHUNK + ATTN_ROWS
    assert (batch * seq) % TOKEN_ROWS == 0

    q_scale = LOG2E / math.sqrt(HEAD_DIM)
    col_scale = jnp.concatenate(
        [jnp.full((1, d), q_scale, _F32), jnp.ones((1, 2 * d), _F32)], axis=1)
    row = lambda g: g.reshape(1, d).astype(_F32)

    x2 = x.reshape(batch * seq, d)
    for i in range(depth):
        qkv = _qkv_proj(x2, row(g_pre_mix[i]), w_qkv[i].astype(_BF16), col_scale)
        if i % 2 == 0:
            o = _chunk_attention(qkv, rel_bias[i // 2].astype(_F32), batch, seq)
        else:
            o = _stick_attention(qkv, batch, seq)
        x2 = _post_mix_ffn(
            o, x2, w_o[i].astype(_BF16), row(g_post_mix[i]), row(g_pre_ffn[i]),
            w_gate[i].astype(_BF16), w_up[i].astype(_BF16), w_down[i].astype(_BF16),
            row(g_post_ffn[i]))
    return x2.reshape(batch, seq, d)
```

```python
import functools
import math

import jax
import jax.numpy as jnp
from jax import lax
from jax.experimental import pallas as pl
from jax.experimental.pallas import tpu as pltpu

HEAD_DIM = 64
CHUNK = 64
LEFT_CHUNKS = 8
REL_CLIP = 128
RMS_EPS = 1e-6
LOG2E = math.log2(math.e)
MASKED_LOGIT = -1e30
DEAD_LOG2_WEIGHT = -140.0
SOFTPLUS_LINEAR_ABOVE = 100.0

V7X_LANES = 128
V7X_MXU_DIM = 256
V7X_VMEM_BYTES = 64 * 1024 * 1024

HEADS_PER_STEP = V7X_MXU_DIM // HEAD_DIM
ATTN_ROWS = V7X_MXU_DIM
STICK_TOP_ROWS = ATTN_ROWS // 2
TOKEN_ROWS = 512

_F32 = jnp.float32
_BF16 = jnp.bfloat16
_CONTRACT_LAST = (((1,), (1,)), ((), ()))


def _vmem_limit(estimate_bytes):
    return int(min(estimate_bytes * 3 // 2, V7X_VMEM_BYTES * 15 // 16))


def _rms_norm(x, g):
    ms = jnp.mean(x * x, axis=-1, keepdims=True)
    return x * lax.rsqrt(ms + RMS_EPS) * g


def _resident(shape):
    return pl.BlockSpec(shape, lambda *_: (0,) * len(shape), pipeline_mode=pl.Buffered(1))


def _qkv_body(x_ref, g_ref, w_ref, s_ref, o_ref):
    h = _rms_norm(x_ref[...], g_ref[...]).astype(_BF16)
    y = jnp.dot(h, w_ref[...], preferred_element_type=_F32)
    o_ref[...] = (y * s_ref[...]).astype(_BF16)


def _qkv_proj(x2, g, w, col_scale):
    t, d = x2.shape
    n = w.shape[1]
    tm = TOKEN_ROWS
    est = 2 * tm * d * 4 + d * n * 2 + 2 * tm * n * 2 + tm * n * 4 + tm * d * 6
    return pl.pallas_call(
        _qkv_body,
        grid=(t // tm,),
        in_specs=[
            pl.BlockSpec((tm, d), lambda i: (i, 0)),
            _resident((1, d)),
            _resident((d, n)),
            _resident((1, n)),
        ],
        out_specs=pl.BlockSpec((tm, n), lambda i: (i, 0)),
        out_shape=jax.ShapeDtypeStruct((t, n), _BF16),
        compiler_params=pltpu.CompilerParams(
            dimension_semantics=("arbitrary",), vmem_limit_bytes=_vmem_limit(est)),
        name="qkv_proj",
    )(x2, g, w, col_scale)


def _head_lanes(width, hh):
    lane = lax.broadcasted_iota(jnp.int32, (1, width), 1)
    return (lane >= hh * HEAD_DIM) & (lane < (hh + 1) * HEAD_DIM)


def _attn_specs(seq, d):
    width = HEADS_PER_STEP * HEAD_DIM
    col_blocks = d // width
    q_spec = pl.BlockSpec((seq, width), lambda hg, b: (b, hg))
    k_spec = pl.BlockSpec((seq, width), lambda hg, b: (b, col_blocks + hg))
    v_spec = pl.BlockSpec((seq, width), lambda hg, b: (b, 2 * col_blocks + hg))
    o_spec = pl.BlockSpec((seq, width), lambda hg, b: (b, hg))
    return q_spec, k_spec, v_spec, o_spec, col_blocks


def _chunk_attn_body(rel_ref, q_ref, k_ref, v_ref, o_ref, bias_ref, *, n_rel, nq):
    hg, b = pl.program_id(0), pl.program_id(1)
    tq = ATTN_ROWS
    nh = HEADS_PER_STEP
    width = HEADS_PER_STEP * HEAD_DIM
    left = LEFT_CHUNKS * CHUNK
    left_blocks = left // tq
    band_blocks = left_blocks + 1
    master_blocks = band_blocks + left_blocks
    master = master_blocks * tq
    roll_w = master + tq

    @pl.when(b == 0)
    def _build_bias():
        wpos = lax.broadcasted_iota(jnp.int32, (8, roll_w), 1)
        ridx = jnp.clip(left + tq - wpos, -(CHUNK - 1), REL_CLIP) + (CHUNK - 1)
        i = lax.broadcasted_iota(jnp.int32, (tq, master), 0)
        u = lax.broadcasted_iota(jnp.int32, (tq, master), 1)
        d = left + i - u
        i_in_chunk = i % CHUNK
        visible = (d >= i_in_chunk - (CHUNK - 1)) & (d <= i_in_chunk + left)
        for hh in range(HEADS_PER_STEP):
            h = hg * HEADS_PER_STEP + hh

            def pick(r, base, h=h):
                return jnp.where(ridx == r, rel_ref[h, r], base)

            base = lax.fori_loop(0, n_rel, pick, jnp.zeros((8, roll_w), _F32))
            rows = jnp.broadcast_to(base[0:1, :] * LOG2E, (tq, roll_w))
            toeplitz = pltpu.roll(rows, master, 1, stride=1, stride_axis=0)
            tile = jnp.where(visible, toeplitz[:, :master], MASKED_LOGIT)
            for ub in range(master_blocks):
                bias_ref[hh, ub] = tile[:, ub * tq:(ub + 1) * tq]

    lane = lax.broadcasted_iota(jnp.int32, (1, width), 1)
    for qi in range(nq):
        start_blk = max(qi - left_blocks, 0)
        n_blocks = qi - start_blk + 1
        bias_blk0 = left_blocks - (qi - start_blk)
        kb = k_ref[start_blk * tq:(qi + 1) * tq, :]
        vb = v_ref[start_blk * tq:(qi + 1) * tq, :]
        q = q_ref[qi * tq:(qi + 1) * tq, :]
        out = None
        for hh in range(nh):
            qm = jnp.where(_head_lanes(width, hh), q, jnp.zeros_like(q))
            z = lax.dot_general(qm, kb, _CONTRACT_LAST, preferred_element_type=_F32)
            z = z + jnp.concatenate(
                [bias_ref[hh, bias_blk0 + t] for t in range(n_blocks)], axis=1)
            p = jnp.exp2(z - jnp.max(z, axis=-1, keepdims=True))
            den = jnp.sum(p, axis=-1, keepdims=True)
            o = jnp.dot(p.astype(_BF16), vb, preferred_element_type=_F32) / den
            out = o if out is None else jnp.where(lane >= hh * HEAD_DIM, o, out)
        o_ref[qi * tq:(qi + 1) * tq, :] = out.astype(_BF16)


def _chunk_attention(qkv, rel_bias, batch, seq):
    t, d3 = qkv.shape
    d = d3 // 3
    tq = ATTN_ROWS
    width = HEADS_PER_STEP * HEAD_DIM
    left_blocks = LEFT_CHUNKS * CHUNK // tq
    master_blocks = 2 * left_blocks + 1
    q_spec, k_spec, v_spec, o_spec, hgs = _attn_specs(seq, d)
    band = (left_blocks + 1) * tq
    bias_bytes = HEADS_PER_STEP * master_blocks * tq * tq * 4
    est = bias_bytes + 8 * seq * width * 2 + 2 * HEADS_PER_STEP * tq * band * 10
    return pl.pallas_call(
        functools.partial(_chunk_attn_body, n_rel=rel_bias.shape[1], nq=seq // tq),
        grid=(hgs, batch),
        in_specs=[pl.BlockSpec(memory_space=pltpu.SMEM), q_spec, k_spec, v_spec],
        out_specs=o_spec,
        out_shape=jax.ShapeDtypeStruct((t, d), _BF16),
        scratch_shapes=[pltpu.VMEM((HEADS_PER_STEP, master_blocks, tq, tq), _F32)],
        compiler_params=pltpu.CompilerParams(
            dimension_semantics=("arbitrary", "arbitrary"),
            vmem_limit_bytes=_vmem_limit(est)),
        name="chunk_attention",
    )(rel_bias, qkv, qkv, qkv)


def _row_norm_lanes(x):
    xf = x.astype(_F32)
    sq = jnp.sum(xf * xf, axis=-1, keepdims=True)
    return jnp.broadcast_to(jnp.sqrt(sq), (x.shape[0], V7X_LANES))


def _stick_attn_body(q_ref, k_ref, v_ref, o_ref, qs_ref, acc_ref, carry_ref, reach_ref, slack_ref,
                     *, nq):
    tq = ATTN_ROWS
    nh = HEADS_PER_STEP
    width = nh * HEAD_DIM
    top = STICK_TOP_ROWS

    r2 = lax.broadcasted_iota(jnp.int32, (tq, tq), 0)
    c2 = lax.broadcasted_iota(jnp.int32, (tq, tq), 1)
    neg_suffix = jnp.where(r2 >= c2, -1.0, 0.0).astype(_BF16)

    k_sq_max = jnp.zeros((tq, V7X_LANES), _F32)
    for j in range(nq):
        kj = k_ref[j * tq:(j + 1) * tq, :].astype(_F32)
        k_sq_max = jnp.maximum(
            k_sq_max, jnp.broadcast_to(jnp.sum(kj * kj, axis=-1, keepdims=True), k_sq_max.shape))
    k_norm_max = jnp.sqrt(jnp.max(k_sq_max))

    def head_stack(q):
        return jnp.concatenate(
            [jnp.where(_head_lanes(width, hh), q, jnp.zeros_like(q)) for hh in range(nh)], axis=0)

    def top_rows(x):
        return jnp.concatenate([x[hh * tq:hh * tq + top] for hh in range(nh)], axis=0)

    def reach_of(q):
        reach = _row_norm_lanes(q) * (k_norm_max * 1.02) + 1.0
        return jnp.concatenate([reach] * nh, axis=0)

    def key_block(q_stack, carry, j, diag):
        n = q_stack.shape[0] // nh
        keys = (slice(j * tq, (j + 1) * tq) if isinstance(j, int)
                else pl.ds(pl.multiple_of(j * tq, tq), tq))
        kj = k_ref[keys, :]
        vj = v_ref[keys, :]
        z = lax.dot_general(q_stack, kj, _CONTRACT_LAST, preferred_element_type=_F32)
        sp = jnp.where(z > SOFTPLUS_LINEAR_ABOVE, z, jnp.log2(1.0 + jnp.exp2(z)))
        if diag:
            row = lax.broadcasted_iota(jnp.int32, z.shape, 0) % n
            col = lax.broadcasted_iota(jnp.int32, z.shape, 1)
            causal = col < row
            sp = jnp.where(causal, sp, 0.0)
        cin = jnp.dot(sp.astype(_BF16), neg_suffix, preferred_element_type=_F32)
        logw = z + cin
        if carry is not None:
            logw = logw + jnp.concatenate([carry] * (tq // V7X_LANES), axis=1)
        a = jnp.exp2(logw)
        if diag:
            a = jnp.where(causal, a, 0.0)
        a = a.astype(_BF16)
        a_cat = jnp.concatenate([a[hh * n:(hh + 1) * n] for hh in range(nh)], axis=1)
        v_cat = jnp.concatenate(
            [jnp.where(_head_lanes(width, hh), vj, jnp.zeros_like(vj)) for hh in range(nh)], axis=0)
        pv = jnp.dot(a_cat, v_cat, preferred_element_type=_F32)
        total = jnp.broadcast_to(cin[:, 0:1], (nh * n, V7X_LANES))
        return pv, (total if carry is None else carry + total)

    def low_rows_slack(slack):
        row = lax.broadcasted_iota(jnp.int32, slack.shape, 0) % tq
        return jnp.max(jnp.where(row >= top, slack, -jnp.inf))

    for qi in range(nq):
        rows = slice(qi * tq, (qi + 1) * tq)
        q = q_ref[rows, :]
        qs = head_stack(q)
        acc, carry = key_block(qs, None, qi, True)
        if qi >= 1:
            pv, carry = key_block(qs, carry, qi - 1, False)
            acc = acc + pv
        if qi >= 2:
            reach = reach_of(q)
            carry_top = top_rows(carry)
            pv, carry_top2 = key_block(top_rows(qs), carry_top, qi - 2, False)
            slack = carry + reach
            slack_low = low_rows_slack(slack)
            low_dead = slack_low <= DEAD_LOG2_WEIGHT
            acc = jnp.concatenate([acc[0:top] + jnp.where(low_dead, pv, 0.0), acc[top:]], axis=0)
            carry_top2 = jnp.where(low_dead, carry_top2, carry_top)
            acc_ref[qi] = acc
            carry_ref[qi] = carry
            for hh in range(nh):
                carry_ref[qi, hh * tq:hh * tq + top, :] = carry_top2[hh * top:(hh + 1) * top]
            slack_ref[qi, 0] = slack_low
            slack_ref[qi, 1] = jnp.max(carry_top2 + top_rows(reach))
            slack_ref[qi, 2] = jnp.max(slack)
        o_ref[rows, :] = acc.astype(_BF16)

    def finish(qi, c):
        low_dead = slack_ref[qi, 0] <= DEAD_LOG2_WEIGHT
        slack_top = slack_ref[qi, 1]
        slack_all = slack_ref[qi, 2]
        more = jnp.where(low_dead, (slack_top > DEAD_LOG2_WEIGHT) & (qi > 2),
                         slack_all > DEAD_LOG2_WEIGHT)

        @pl.when(more)
        def _visit_more():
            rows = pl.ds(pl.multiple_of(qi * tq, tq), tq)
            q = q_ref[rows, :]
            qs_ref[...] = head_stack(q)
            reach_ref[...] = reach_of(q)

            def top_alive(state):
                step, slack = state
                return (step < qi) & (slack > DEAD_LOG2_WEIGHT) & low_dead

            def top_step(state):
                step, _ = state
                pv, carry = key_block(top_rows(qs_ref), top_rows(carry_ref.at[qi]),
                                      qi - 1 - step, False)
                acc_ref[qi, 0:top, :] += pv
                for hh in range(nh):
                    carry_ref[qi, hh * tq:hh * tq + top, :] = carry[hh * top:(hh + 1) * top]
                return step + 1, jnp.max(carry + top_rows(reach_ref))

            lax.while_loop(top_alive, top_step, (jnp.int32(2), slack_top))

            def all_alive(state):
                step, slack = state
                return (step < qi) & (slack > DEAD_LOG2_WEIGHT) & jnp.logical_not(low_dead)

            def all_step(state):
                step, _ = state
                pv, carry = key_block(qs_ref[...], carry_ref[qi], qi - 1 - step, False)
                acc_ref[qi] += pv
                carry_ref[qi] = carry
                return step + 1, jnp.max(carry + reach_ref[...])

            lax.while_loop(all_alive, all_step, (jnp.int32(1), slack_all))
            o_ref[rows, :] = acc_ref[qi].astype(_BF16)

        return c

    lax.fori_loop(2, nq, finish, 0)


def _stick_attention(qkv, batch, seq):
    t, d3 = qkv.shape
    d = d3 // 3
    tq = ATTN_ROWS
    nh = HEADS_PER_STEP
    width = nh * HEAD_DIM
    nq = seq // tq
    q_spec, k_spec, v_spec, o_spec, hgs = _attn_specs(seq, d)
    est = (8 * seq * width * 2 + nh * tq * width * 2 + nq * tq * width * 4
           + (nq + 1) * nh * tq * V7X_LANES * 4 + 16 * nh * tq * tq * 4)
    return pl.pallas_call(
        functools.partial(_stick_attn_body, nq=seq // tq),
        grid=(hgs, batch),
        in_specs=[q_spec, k_spec, v_spec],
        out_specs=o_spec,
        out_shape=jax.ShapeDtypeStruct((t, d), _BF16),
        scratch_shapes=[
            pltpu.VMEM((nh * tq, width), _BF16),
            pltpu.VMEM((nq, tq, width), _F32),
            pltpu.VMEM((nq, nh * tq, V7X_LANES), _F32),
            pltpu.VMEM((nh * tq, V7X_LANES), _F32),
            pltpu.SMEM((nq, 3), _F32),
        ],
        compiler_params=pltpu.CompilerParams(
            dimension_semantics=("arbitrary", "arbitrary"),
            vmem_limit_bytes=_vmem_limit(est)),
        name="stick_attention",
    )(qkv, qkv, qkv)


def _post_body(o_ref, x_ref, wo_ref, g_post_mix_ref, g_pre_ffn_ref, wg_ref, wu_ref, wd_ref,
               g_post_ffn_ref, out_ref):
    m = jnp.dot(o_ref[...], wo_ref[...], preferred_element_type=_F32)
    x1 = x_ref[...] + _rms_norm(m, g_post_mix_ref[...])
    h = _rms_norm(x1, g_pre_ffn_ref[...]).astype(_BF16)
    gate = jnp.dot(h, wg_ref[...], preferred_element_type=_F32)
    up = jnp.dot(h, wu_ref[...], preferred_element_type=_F32)
    act = (gate * jax.nn.sigmoid(gate) * up).astype(_BF16)
    y = jnp.dot(act, wd_ref[...], preferred_element_type=_F32)
    out_ref[...] = x1 + _rms_norm(y, g_post_ffn_ref[...])


def _post_mix_ffn(o, x2, wo, g_post_mix, g_pre_ffn, wg, wu, wd, g_post_ffn):
    t, d = x2.shape
    f = wg.shape[1]
    tm = TOKEN_ROWS
    est = (d * d + 3 * d * f) * 2 + 2 * tm * d * (2 + 4 + 4) + tm * (3 * d * 4 + 2 * f * 4 + f * 2 + d * 2)
    row = lambda i: (i, 0)
    return pl.pallas_call(
        _post_body,
        grid=(t // tm,),
        in_specs=[
            pl.BlockSpec((tm, d), row),
            pl.BlockSpec((tm, d), row),
            _resident((d, d)),
            _resident((1, d)),
            _resident((1, d)),
            _resident((d, f)),
            _resident((d, f)),
            _resident((f, d)),
            _resident((1, d)),
        ],
        out_specs=pl.BlockSpec((tm, d), row),
        out_shape=jax.ShapeDtypeStruct((t, d), _F32),
        compiler_params=pltpu.CompilerParams(
            dimension_semantics=("arbitrary",), vmem_limit_bytes=_vmem_limit(est)),
        name="post_mix_ffn",
    )(o, x2, wo, g_post_mix, g_pre_ffn, wg, wu, wd, g_post_ffn)


def kernel(x, g_pre_mix, g_post_mix, w_qkv, w_o, rel_bias, g_pre_ffn, g_post_ffn, w_gate, w_up, w_down):
    batch, seq, d = x.shape
    depth = w_qkv.shape[0]
    assert d % (HEADS_PER_STEP * HEAD_DIM) == 0 and seq % ATTN_ROWS == 0
    assert (LEFT_CHUNKS * CHUNK) % ATTN_ROWS == 0 and seq >= LEFT_CHUNKS * CHUNK + ATTN_ROWS
    assert (batch * seq) % TOKEN_ROWS == 0

    q_scale = LOG2E / math.sqrt(HEAD_DIM)
    col_scale = jnp.concatenate(
        [jnp.full((1, d), q_scale, _F32), jnp.ones((1, 2 * d), _F32)], axis=1)
    row = lambda g: g.reshape(1, d).astype(_F32)

    x2 = x.reshape(batch * seq, d)
    for i in range(depth):
        qkv = _qkv_proj(x2, row(g_pre_mix[i]), w_qkv[i].astype(_BF16), col_scale)
        if i % 2 == 0:
            o = _chunk_attention(qkv, rel_bias[i // 2].astype(_F32), batch, seq)
        else:
            o = _stick_attention(qkv, batch, seq)
        x2 = _post_mix_ffn(
            o, x2, w_o[i].astype(_BF16), row(g_post_mix[i]), row(g_pre_ffn[i]),
            w_gate[i].astype(_BF16), w_up[i].astype(_BF16), w_down[i].astype(_BF16),
            row(g_post_ffn[i]))
    return x2.reshape(batch, seq, d)
```

```python
import functools
import math

import jax
import jax.numpy as jnp
from jax import lax
from jax.experimental import pallas as pl
from jax.experimental.pallas import tpu as pltpu

HEAD_DIM = 64
CHUNK = 64
LEFT_CHUNKS = 8
REL_CLIP = 128
RMS_EPS = 1e-6
LOG2E = math.log2(math.e)
MASKED_LOGIT = -1e30
DEAD_LOG2_WEIGHT = -140.0
SOFTPLUS_LINEAR_ABOVE = 100.0

V7X_LANES = 128
V7X_MXU_DIM = 256
V7X_VMEM_BYTES = 64 * 1024 * 1024

HEADS_PER_STEP = V7X_MXU_DIM // HEAD_DIM
ATTN_ROWS = V7X_MXU_DIM
STICK_TOP_ROWS = ATTN_ROWS // 2
TOKEN_ROWS = 1024

_F32 = jnp.float32
_BF16 = jnp.bfloat16
_CONTRACT_LAST = (((1,), (1,)), ((), ()))


def _vmem_limit(estimate_bytes):
    return int(min(estimate_bytes * 3 // 2, V7X_VMEM_BYTES * 15 // 16))


def _rms_norm(x, g):
    ms = jnp.mean(x * x, axis=-1, keepdims=True)
    return x * lax.rsqrt(ms + RMS_EPS) * g


def _resident(shape, layer=None):
    if layer is None:
        return pl.BlockSpec(shape, lambda *_: (0,) * len(shape), pipeline_mode=pl.Buffered(1))
    return pl.BlockSpec((None,) + shape, lambda *_: (layer,) + (0,) * len(shape),
                        pipeline_mode=pl.Buffered(1))


def _qkv_body(x_ref, g_ref, w_ref, s_ref, o_ref):
    h = _rms_norm(x_ref[...], g_ref[...]).astype(_BF16)
    y = jnp.dot(h, w_ref[...], preferred_element_type=_F32)
    o_ref[...] = (y * s_ref[...]).astype(_BF16)


def _qkv_proj(x2, g, w, col_scale, layer):
    t, d = x2.shape
    n = w.shape[2]
    tm = TOKEN_ROWS
    est = 2 * tm * d * 4 + d * n * 2 + 2 * tm * n * 2 + tm * n * 4 + tm * d * 6
    return pl.pallas_call(
        _qkv_body,
        grid=(t // tm,),
        in_specs=[
            pl.BlockSpec((tm, d), lambda i: (i, 0)),
            _resident((1, d), layer),
            _resident((d, n), layer),
            _resident((1, n)),
        ],
        out_specs=pl.BlockSpec((tm, n), lambda i: (i, 0)),
        out_shape=jax.ShapeDtypeStruct((t, n), _BF16),
        compiler_params=pltpu.CompilerParams(
            dimension_semantics=("arbitrary",), vmem_limit_bytes=_vmem_limit(est)),
        name="qkv_proj",
    )(x2, g, w, col_scale)


def _head_lanes(width, hh):
    lane = lax.broadcasted_iota(jnp.int32, (1, width), 1)
    return (lane >= hh * HEAD_DIM) & (lane < (hh + 1) * HEAD_DIM)


def _attn_specs(seq, d):
    width = HEADS_PER_STEP * HEAD_DIM
    col_blocks = d // width
    q_spec = pl.BlockSpec((seq, width), lambda hg, b: (b, hg))
    k_spec = pl.BlockSpec((seq, width), lambda hg, b: (b, col_blocks + hg))
    v_spec = pl.BlockSpec((seq, width), lambda hg, b: (b, 2 * col_blocks + hg))
    o_spec = pl.BlockSpec((seq, width), lambda hg, b: (b, hg))
    return q_spec, k_spec, v_spec, o_spec, col_blocks


def _chunk_attn_body(rel_ref, q_ref, k_ref, v_ref, o_ref, bias_ref, *, n_rel, nq):
    hg, b = pl.program_id(0), pl.program_id(1)
    tq = ATTN_ROWS
    nh = HEADS_PER_STEP
    width = HEADS_PER_STEP * HEAD_DIM
    left = LEFT_CHUNKS * CHUNK
    left_blocks = left // tq
    band_blocks = left_blocks + 1
    master_blocks = band_blocks + left_blocks
    master = master_blocks * tq
    roll_w = master + tq

    @pl.when(b == 0)
    def _build_bias():
        wpos = lax.broadcasted_iota(jnp.int32, (8, roll_w), 1)
        ridx = jnp.clip(left + tq - wpos, -(CHUNK - 1), REL_CLIP) + (CHUNK - 1)
        i = lax.broadcasted_iota(jnp.int32, (tq, master), 0)
        u = lax.broadcasted_iota(jnp.int32, (tq, master), 1)
        d = left + i - u
        i_in_chunk = i % CHUNK
        visible = (d >= i_in_chunk - (CHUNK - 1)) & (d <= i_in_chunk + left)
        for hh in range(HEADS_PER_STEP):
            h = hg * HEADS_PER_STEP + hh

            def pick(r, base, h=h):
                return jnp.where(ridx == r, rel_ref[h, r], base)

            base = lax.fori_loop(0, n_rel, pick, jnp.zeros((8, roll_w), _F32))
            rows = jnp.broadcast_to(base[0:1, :] * LOG2E, (tq, roll_w))
            toeplitz = pltpu.roll(rows, master, 1, stride=1, stride_axis=0)
            tile = jnp.where(visible, toeplitz[:, :master], MASKED_LOGIT)
            for ub in range(master_blocks):
                bias_ref[hh, ub] = tile[:, ub * tq:(ub + 1) * tq]

    lane = lax.broadcasted_iota(jnp.int32, (1, width), 1)
    for qi in range(nq):
        start_blk = max(qi - left_blocks, 0)
        n_blocks = qi - start_blk + 1
        bias_blk0 = left_blocks - (qi - start_blk)
        kb = k_ref[start_blk * tq:(qi + 1) * tq, :]
        vb = v_ref[start_blk * tq:(qi + 1) * tq, :]
        q = q_ref[qi * tq:(qi + 1) * tq, :]
        out = None
        for hh in range(nh):
            qm = jnp.where(_head_lanes(width, hh), q, jnp.zeros_like(q))
            z = lax.dot_general(qm, kb, _CONTRACT_LAST, preferred_element_type=_F32)
            z = z + jnp.concatenate(
                [bias_ref[hh, bias_blk0 + t] for t in range(n_blocks)], axis=1)
            p = jnp.exp2(z - jnp.max(z, axis=-1, keepdims=True))
            den = jnp.sum(p, axis=-1, keepdims=True)
            o = jnp.dot(p.astype(_BF16), vb, preferred_element_type=_F32) / den
            out = o if out is None else jnp.where(lane >= hh * HEAD_DIM, o, out)
        o_ref[qi * tq:(qi + 1) * tq, :] = out.astype(_BF16)


def _chunk_attention(qkv, rel_bias, batch, seq):
    t, d3 = qkv.shape
    d = d3 // 3
    tq = ATTN_ROWS
    width = HEADS_PER_STEP * HEAD_DIM
    left_blocks = LEFT_CHUNKS * CHUNK // tq
    master_blocks = 2 * left_blocks + 1
    q_spec, k_spec, v_spec, o_spec, hgs = _attn_specs(seq, d)
    band = (left_blocks + 1) * tq
    bias_bytes = HEADS_PER_STEP * master_blocks * tq * tq * 4
    est = bias_bytes + 8 * seq * width * 2 + 2 * HEADS_PER_STEP * tq * band * 10
    return pl.pallas_call(
        functools.partial(_chunk_attn_body, n_rel=rel_bias.shape[1], nq=seq // tq),
        grid=(hgs, batch),
        in_specs=[pl.BlockSpec(memory_space=pltpu.SMEM), q_spec, k_spec, v_spec],
        out_specs=o_spec,
        out_shape=jax.ShapeDtypeStruct((t, d), _BF16),
        scratch_shapes=[pltpu.VMEM((HEADS_PER_STEP, master_blocks, tq, tq), _F32)],
        compiler_params=pltpu.CompilerParams(
            dimension_semantics=("arbitrary", "arbitrary"),
            vmem_limit_bytes=_vmem_limit(est)),
        name="chunk_attention",
    )(rel_bias, qkv, qkv, qkv)


def _row_norm_lanes(x):
    xf = x.astype(_F32)
    sq = jnp.sum(xf * xf, axis=-1, keepdims=True)
    return jnp.broadcast_to(jnp.sqrt(sq), (x.shape[0], V7X_LANES))


def _stick_attn_body(q_ref, k_ref, v_ref, o_ref, qs_ref, acc_ref, carry_ref, reach_ref, slack_ref,
                     *, nq):
    tq = ATTN_ROWS
    nh = HEADS_PER_STEP
    width = nh * HEAD_DIM
    top = STICK_TOP_ROWS

    r2 = lax.broadcasted_iota(jnp.int32, (tq, tq), 0)
    c2 = lax.broadcasted_iota(jnp.int32, (tq, tq), 1)
    neg_suffix = jnp.where(r2 >= c2, -1.0, 0.0).astype(_BF16)

    k_sq_max = jnp.zeros((tq, V7X_LANES), _F32)
    for j in range(nq):
        kj = k_ref[j * tq:(j + 1) * tq, :].astype(_F32)
        k_sq_max = jnp.maximum(
            k_sq_max, jnp.broadcast_to(jnp.sum(kj * kj, axis=-1, keepdims=True), k_sq_max.shape))
    k_norm_max = jnp.sqrt(jnp.max(k_sq_max))

    def head_stack(q):
        return jnp.concatenate(
            [jnp.where(_head_lanes(width, hh), q, jnp.zeros_like(q)) for hh in range(nh)], axis=0)

    def top_rows(x):
        return jnp.concatenate([x[hh * tq:hh * tq + top] for hh in range(nh)], axis=0)

    def reach_of(q):
        reach = _row_norm_lanes(q) * (k_norm_max * 1.02) + 1.0
        return jnp.concatenate([reach] * nh, axis=0)

    def key_block(q_stack, carry, j, diag):
        n = q_stack.shape[0] // nh
        keys = (slice(j * tq, (j + 1) * tq) if isinstance(j, int)
                else pl.ds(pl.multiple_of(j * tq, tq), tq))
        kj = k_ref[keys, :]
        vj = v_ref[keys, :]
        z = lax.dot_general(q_stack, kj, _CONTRACT_LAST, preferred_element_type=_F32)
        sp = jnp.where(z > SOFTPLUS_LINEAR_ABOVE, z, jnp.log2(1.0 + jnp.exp2(z)))
        if diag:
            row = lax.broadcasted_iota(jnp.int32, z.shape, 0) % n
            col = lax.broadcasted_iota(jnp.int32, z.shape, 1)
            causal = col < row
            sp = jnp.where(causal, sp, 0.0)
        cin = jnp.dot(sp.astype(_BF16), neg_suffix, preferred_element_type=_F32)
        logw = z + cin
        if carry is not None:
            logw = logw + jnp.concatenate([carry] * (tq // V7X_LANES), axis=1)
        a = jnp.exp2(logw)
        if diag:
            a = jnp.where(causal, a, 0.0)
        a = a.astype(_BF16)
        a_cat = jnp.concatenate([a[hh * n:(hh + 1) * n] for hh in range(nh)], axis=1)
        v_cat = jnp.concatenate(
            [jnp.where(_head_lanes(width, hh), vj, jnp.zeros_like(vj)) for hh in range(nh)], axis=0)
        pv = jnp.dot(a_cat, v_cat, preferred_element_type=_F32)
        total = jnp.broadcast_to(cin[:, 0:1], (nh * n, V7X_LANES))
        return pv, (total if carry is None else carry + total)

    def low_rows_slack(slack):
        row = lax.broadcasted_iota(jnp.int32, slack.shape, 0) % tq
        return jnp.max(jnp.where(row >= top, slack, -jnp.inf))

    for qi in range(nq):
        rows = slice(qi * tq, (qi + 1) * tq)
        q = q_ref[rows, :]
        qs = head_stack(q)
        acc, carry = key_block(qs, None, qi, True)
        if qi >= 1:
            pv, carry = key_block(qs, carry, qi - 1, False)
            acc = acc + pv
        if qi >= 2:
            reach = reach_of(q)
            carry_top = top_rows(carry)
            pv, carry_top2 = key_block(top_rows(qs), carry_top, qi - 2, False)
            slack = carry + reach
            slack_low = low_rows_slack(slack)
            low_dead = slack_low <= DEAD_LOG2_WEIGHT
            acc = jnp.concatenate([acc[0:top] + jnp.where(low_dead, pv, 0.0), acc[top:]], axis=0)
            carry_top2 = jnp.where(low_dead, carry_top2, carry_top)
            acc_ref[qi] = acc
            carry_ref[qi] = carry
            for hh in range(nh):
                carry_ref[qi, hh * tq:hh * tq + top, :] = carry_top2[hh * top:(hh + 1) * top]
            slack_ref[qi, 0] = slack_low
            slack_ref[qi, 1] = jnp.max(carry_top2 + top_rows(reach))
            slack_ref[qi, 2] = jnp.max(slack)
        o_ref[rows, :] = acc.astype(_BF16)

    def finish(qi, c):
        low_dead = slack_ref[qi, 0] <= DEAD_LOG2_WEIGHT
        slack_top = slack_ref[qi, 1]
        slack_all = slack_ref[qi, 2]
        more = jnp.where(low_dead, (slack_top > DEAD_LOG2_WEIGHT) & (qi > 2),
                         slack_all > DEAD_LOG2_WEIGHT)

        @pl.when(more)
        def _visit_more():
            rows = pl.ds(pl.multiple_of(qi * tq, tq), tq)
            q = q_ref[rows, :]
            qs_ref[...] = head_stack(q)
            reach_ref[...] = reach_of(q)

            def top_alive(state):
                step, slack = state
                return (step < qi) & (slack > DEAD_LOG2_WEIGHT) & low_dead

            def top_step(state):
                step, _ = state
                pv, carry = key_block(top_rows(qs_ref), top_rows(carry_ref.at[qi]),
                                      qi - 1 - step, False)
                acc_ref[qi, 0:top, :] += pv
                for hh in range(nh):
                    carry_ref[qi, hh * tq:hh * tq + top, :] = carry[hh * top:(hh + 1) * top]
                return step + 1, jnp.max(carry + top_rows(reach_ref))

            lax.while_loop(top_alive, top_step, (jnp.int32(2), slack_top))

            def all_alive(state):
                step, slack = state
                return (step < qi) & (slack > DEAD_LOG2_WEIGHT) & jnp.logical_not(low_dead)

            def all_step(state):
                step, _ = state
                pv, carry = key_block(qs_ref[...], carry_ref[qi], qi - 1 - step, False)
                acc_ref[qi] += pv
                carry_ref[qi] = carry
                return step + 1, jnp.max(carry + reach_ref[...])

            lax.while_loop(all_alive, all_step, (jnp.int32(1), slack_all))
            o_ref[rows, :] = acc_ref[qi].astype(_BF16)

        return c

    lax.fori_loop(2, nq, finish, 0)


def _stick_attention(qkv, batch, seq):
    t, d3 = qkv.shape
    d = d3 // 3
    tq = ATTN_ROWS
    nh = HEADS_PER_STEP
    width = nh * HEAD_DIM
    nq = seq // tq
    q_spec, k_spec, v_spec, o_spec, hgs = _attn_specs(seq, d)
    est = (8 * seq * width * 2 + nh * tq * width * 2 + nq * tq * width * 4
           + (nq + 1) * nh * tq * V7X_LANES * 4 + 16 * nh * tq * tq * 4)
    return pl.pallas_call(
        functools.partial(_stick_attn_body, nq=seq // tq),
        grid=(hgs, batch),
        in_specs=[q_spec, k_spec, v_spec],
        out_specs=o_spec,
        out_shape=jax.ShapeDtypeStruct((t, d), _BF16),
        scratch_shapes=[
            pltpu.VMEM((nh * tq, width), _BF16),
            pltpu.VMEM((nq, tq, width), _F32),
            pltpu.VMEM((nq, nh * tq, V7X_LANES), _F32),
            pltpu.VMEM((nh * tq, V7X_LANES), _F32),
            pltpu.SMEM((nq, 3), _F32),
        ],
        compiler_params=pltpu.CompilerParams(
            dimension_semantics=("arbitrary", "arbitrary"),
            vmem_limit_bytes=_vmem_limit(est)),
        name="stick_attention",
    )(qkv, qkv, qkv)


def _post_body(o_ref, x_ref, wo_ref, g_post_mix_ref, g_pre_ffn_ref, wg_ref, wu_ref, wd_ref,
               g_post_ffn_ref, out_ref):
    m = jnp.dot(o_ref[...], wo_ref[...], preferred_element_type=_F32)
    x1 = x_ref[...] + _rms_norm(m, g_post_mix_ref[...])
    h = _rms_norm(x1, g_pre_ffn_ref[...]).astype(_BF16)
    gate = jnp.dot(h, wg_ref[...], preferred_element_type=_F32)
    up = jnp.dot(h, wu_ref[...], preferred_element_type=_F32)
    act = (gate * jax.nn.sigmoid(gate) * up).astype(_BF16)
    y = jnp.dot(act, wd_ref[...], preferred_element_type=_F32)
    out_ref[...] = x1 + _rms_norm(y, g_post_ffn_ref[...])


def _post_mix_ffn(o, x2, wo, g_post_mix, g_pre_ffn, wg, wu, wd, g_post_ffn, layer):
    t, d = x2.shape
    f = wg.shape[2]
    tm = TOKEN_ROWS
    est = (d * d + 3 * d * f) * 2 + 2 * tm * d * (2 + 4 + 4) + tm * (3 * d * 4 + 2 * f * 4 + f * 2 + d * 2)
    row = lambda i: (i, 0)
    return pl.pallas_call(
        _post_body,
        grid=(t // tm,),
        in_specs=[
            pl.BlockSpec((tm, d), row),
            pl.BlockSpec((tm, d), row),
            _resident((d, d), layer),
            _resident((1, d), layer),
            _resident((1, d), layer),
            _resident((d, f), layer),
            _resident((d, f), layer),
            _resident((f, d), layer),
            _resident((1, d), layer),
        ],
        out_specs=pl.BlockSpec((tm, d), row),
        out_shape=jax.ShapeDtypeStruct((t, d), _F32),
        compiler_params=pltpu.CompilerParams(
            dimension_semantics=("arbitrary",), vmem_limit_bytes=_vmem_limit(est)),
        name="post_mix_ffn",
    )(o, x2, wo, g_post_mix, g_pre_ffn, wg, wu, wd, g_post_ffn)


def kernel(x, g_pre_mix, g_post_mix, w_qkv, w_o, rel_bias, g_pre_ffn, g_post_ffn, w_gate, w_up, w_down):
    batch, seq, d = x.shape
    depth = w_qkv.shape[0]
    assert d % (HEADS_PER_STEP * HEAD_DIM) == 0 and seq % ATTN_ROWS == 0
    assert (LEFT_CHUNKS * CHUNK) % ATTN_ROWS == 0 and seq >= LEFT_CHUNKS * CHUNK + ATTN_ROWS
    assert (batch * seq) % TOKEN_ROWS == 0

    q_scale = LOG2E / math.sqrt(HEAD_DIM)
    col_scale = jnp.concatenate(
        [jnp.full((1, d), q_scale, _F32), jnp.ones((1, 2 * d), _F32)], axis=1)
    gains = lambda g: g.reshape(depth, 1, d).astype(_F32)
    g_pre_mix, g_post_mix, g_pre_ffn, g_post_ffn = map(
        gains, (g_pre_mix, g_post_mix, g_pre_ffn, g_post_ffn))
    w_qkv, w_o, w_gate, w_up, w_down = (
        w.astype(_BF16) for w in (w_qkv, w_o, w_gate, w_up, w_down))

    x2 = x.reshape(batch * seq, d)
    for i in range(depth):
        qkv = _qkv_proj(x2, g_pre_mix, w_qkv, col_scale, i)
        if i % 2 == 0:
            o = _chunk_attention(qkv, rel_bias[i // 2].astype(_F32), batch, seq)
        else:
            o = _stick_attention(qkv, batch, seq)
        x2 = _post_mix_ffn(o, x2, w_o, g_post_mix, g_pre_ffn, w_gate, w_up, w_down, g_post_ffn, i)
    return x2.reshape(batch, seq, d)
```

```python
import functools
import math

import jax
import jax.numpy as jnp
from jax import lax
from jax.experimental import pallas as pl
from jax.experimental.pallas import tpu as pltpu

HEAD_DIM = 64
CHUNK = 64
LEFT_CHUNKS = 8
REL_CLIP = 128
RMS_EPS = 1e-6
LOG2E = math.log2(math.e)
MASKED_LOGIT = -1e30
DEAD_LOG2_WEIGHT = -140.0
SOFTPLUS_LINEAR_ABOVE = 100.0

V7X_LANES = 128
V7X_MXU_DIM = 256
V7X_VMEM_BYTES = 64 * 1024 * 1024

HEADS_PER_STEP = V7X_MXU_DIM // HEAD_DIM
ATTN_ROWS = V7X_MXU_DIM
STICK_TOP_ROWS = ATTN_ROWS // 2
TOKEN_ROWS = 1024

_F32 = jnp.float32
_BF16 = jnp.bfloat16
_CONTRACT_LAST = (((1,), (1,)), ((), ()))


def _vmem_limit(estimate_bytes):
    return int(min(estimate_bytes * 3 // 2, V7X_VMEM_BYTES * 15 // 16))


def _rms_norm(x, g):
    ms = jnp.mean(x * x, axis=-1, keepdims=True)
    return x * lax.rsqrt(ms + RMS_EPS) * g


def _resident(shape, layer=None):
    if layer is None:
        return pl.BlockSpec(shape, lambda *_: (0,) * len(shape), pipeline_mode=pl.Buffered(1))
    return pl.BlockSpec((None,) + shape, lambda *_: (layer,) + (0,) * len(shape),
                        pipeline_mode=pl.Buffered(1))


def _qkv_body(x_ref, g_ref, w_ref, s_ref, o_ref):
    h = _rms_norm(x_ref[...], g_ref[...]).astype(_BF16)
    y = jnp.dot(h, w_ref[...], preferred_element_type=_F32)
    o_ref[...] = (y * s_ref[...]).astype(_BF16)


def _qkv_proj(x2, g, w, col_scale, layer):
    t, d = x2.shape
    n = w.shape[2]
    tm = TOKEN_ROWS
    est = 2 * tm * d * 4 + d * n * 2 + 2 * tm * n * 2 + tm * n * 4 + tm * d * 6
    return pl.pallas_call(
        _qkv_body,
        grid=(t // tm,),
        in_specs=[
            pl.BlockSpec((tm, d), lambda i: (i, 0)),
            _resident((1, d), layer),
            _resident((d, n), layer),
            _resident((1, n)),
        ],
        out_specs=pl.BlockSpec((tm, n), lambda i: (i, 0)),
        out_shape=jax.ShapeDtypeStruct((t, n), _BF16),
        compiler_params=pltpu.CompilerParams(
            dimension_semantics=("arbitrary",), vmem_limit_bytes=_vmem_limit(est)),
        name="qkv_proj",
    )(x2, g, w, col_scale)


def _head_lanes(width, hh):
    lane = lax.broadcasted_iota(jnp.int32, (1, width), 1)
    return (lane >= hh * HEAD_DIM) & (lane < (hh + 1) * HEAD_DIM)


def _attn_specs(seq, d):
    width = HEADS_PER_STEP * HEAD_DIM
    col_blocks = d // width
    q_spec = pl.BlockSpec((seq, width), lambda hg, b: (b, hg))
    k_spec = pl.BlockSpec((seq, width), lambda hg, b: (b, col_blocks + hg))
    v_spec = pl.BlockSpec((seq, width), lambda hg, b: (b, 2 * col_blocks + hg))
    o_spec = pl.BlockSpec((seq, width), lambda hg, b: (b, hg))
    return q_spec, k_spec, v_spec, o_spec, col_blocks


def _chunk_attn_body(rel_ref, q_ref, k_ref, v_ref, o_ref, bias_ref, *, n_rel, nq):
    hg, b = pl.program_id(0), pl.program_id(1)
    tq = ATTN_ROWS
    nh = HEADS_PER_STEP
    width = HEADS_PER_STEP * HEAD_DIM
    left = LEFT_CHUNKS * CHUNK
    left_blocks = left // tq
    band_blocks = left_blocks + 1
    master_blocks = band_blocks + left_blocks
    master = master_blocks * tq
    roll_w = master + tq

    @pl.when(b == 0)
    def _build_bias():
        wpos = lax.broadcasted_iota(jnp.int32, (8, roll_w), 1)
        ridx = jnp.clip(left + tq - wpos, -(CHUNK - 1), REL_CLIP) + (CHUNK - 1)
        i = lax.broadcasted_iota(jnp.int32, (tq, master), 0)
        u = lax.broadcasted_iota(jnp.int32, (tq, master), 1)
        d = left + i - u
        i_in_chunk = i % CHUNK
        visible = (d >= i_in_chunk - (CHUNK - 1)) & (d <= i_in_chunk + left)
        for hh in range(HEADS_PER_STEP):
            h = hg * HEADS_PER_STEP + hh

            def pick(r, base, h=h):
                return jnp.where(ridx == r, rel_ref[h, r], base)

            base = lax.fori_loop(0, n_rel, pick, jnp.zeros((8, roll_w), _F32))
            rows = jnp.broadcast_to(base[0:1, :] * LOG2E, (tq, roll_w))
            toeplitz = pltpu.roll(rows, master, 1, stride=1, stride_axis=0)
            tile = jnp.where(visible, toeplitz[:, :master], MASKED_LOGIT)
            for ub in range(master_blocks):
                bias_ref[hh, ub] = tile[:, ub * tq:(ub + 1) * tq]

    lane = lax.broadcasted_iota(jnp.int32, (1, width), 1)
    for qi in range(nq):
        start_blk = max(qi - left_blocks, 0)
        n_blocks = qi - start_blk + 1
        bias_blk0 = left_blocks - (qi - start_blk)
        kb = k_ref[start_blk * tq:(qi + 1) * tq, :]
        vb = v_ref[start_blk * tq:(qi + 1) * tq, :]
        q = q_ref[qi * tq:(qi + 1) * tq, :]
        out = None
        for hh in range(nh):
            qm = jnp.where(_head_lanes(width, hh), q, jnp.zeros_like(q))
            z = lax.dot_general(qm, kb, _CONTRACT_LAST, preferred_element_type=_F32)
            z = z + jnp.concatenate(
                [bias_ref[hh, bias_blk0 + t] for t in range(n_blocks)], axis=1)
            p = jnp.exp2(z - jnp.max(z, axis=-1, keepdims=True))
            den = jnp.sum(p, axis=-1, keepdims=True)
            o = jnp.dot(p.astype(_BF16), vb, preferred_element_type=_F32) / den
            out = o if out is None else jnp.where(lane >= hh * HEAD_DIM, o, out)
        o_ref[qi * tq:(qi + 1) * tq, :] = out.astype(_BF16)


def _chunk_attention(qkv, rel_bias, batch, seq):
    t, d3 = qkv.shape
    d = d3 // 3
    tq = ATTN_ROWS
    width = HEADS_PER_STEP * HEAD_DIM
    left_blocks = LEFT_CHUNKS * CHUNK // tq
    master_blocks = 2 * left_blocks + 1
    q_spec, k_spec, v_spec, o_spec, hgs = _attn_specs(seq, d)
    band = (left_blocks + 1) * tq
    bias_bytes = HEADS_PER_STEP * master_blocks * tq * tq * 4
    est = bias_bytes + 8 * seq * width * 2 + 2 * HEADS_PER_STEP * tq * band * 10
    return pl.pallas_call(
        functools.partial(_chunk_attn_body, n_rel=rel_bias.shape[1], nq=seq // tq),
        grid=(hgs, batch),
        in_specs=[pl.BlockSpec(memory_space=pltpu.SMEM), q_spec, k_spec, v_spec],
        out_specs=o_spec,
        out_shape=jax.ShapeDtypeStruct((t, d), _BF16),
        scratch_shapes=[pltpu.VMEM((HEADS_PER_STEP, master_blocks, tq, tq), _F32)],
        compiler_params=pltpu.CompilerParams(
            dimension_semantics=("arbitrary", "arbitrary"),
            vmem_limit_bytes=_vmem_limit(est)),
        name="chunk_attention",
    )(rel_bias, qkv, qkv, qkv)


def _row_norm_lanes(x):
    xf = x.astype(_F32)
    sq = jnp.sum(xf * xf, axis=-1, keepdims=True)
    return jnp.broadcast_to(jnp.sqrt(sq), (x.shape[0], V7X_LANES))


def _stick_attn_body(q_ref, k_ref, v_ref, o_ref, qs_ref, acc_ref, carry_ref, reach_ref, slack_ref,
                     *, nq):
    tq = ATTN_ROWS
    nh = HEADS_PER_STEP
    width = nh * HEAD_DIM
    top = STICK_TOP_ROWS

    r2 = lax.broadcasted_iota(jnp.int32, (tq, tq), 0)
    c2 = lax.broadcasted_iota(jnp.int32, (tq, tq), 1)
    neg_suffix = jnp.where(r2 > c2, -1.0, 0.0).astype(_BF16)

    k_sq_max = jnp.zeros((tq, V7X_LANES), _F32)
    for j in range(nq):
        kj = k_ref[j * tq:(j + 1) * tq, :].astype(_F32)
        k_sq_max = jnp.maximum(
            k_sq_max, jnp.broadcast_to(jnp.sum(kj * kj, axis=-1, keepdims=True), k_sq_max.shape))
    k_norm_max = jnp.sqrt(jnp.max(k_sq_max))

    def head_stack(q):
        return jnp.concatenate(
            [jnp.where(_head_lanes(width, hh), q, jnp.zeros_like(q)) for hh in range(nh)], axis=0)

    def top_rows(x):
        return jnp.concatenate([x[hh * tq:hh * tq + top] for hh in range(nh)], axis=0)

    def reach_of(q):
        reach = _row_norm_lanes(q) * (k_norm_max * 1.02) + 1.0
        return jnp.concatenate([reach] * nh, axis=0)

    def key_block(q_stack, carry, j, diag):
        n = q_stack.shape[0] // nh
        keys = (slice(j * tq, (j + 1) * tq) if isinstance(j, int)
                else pl.ds(pl.multiple_of(j * tq, tq), tq))
        kj = k_ref[keys, :]
        vj = v_ref[keys, :]
        z = lax.dot_general(q_stack, kj, _CONTRACT_LAST, preferred_element_type=_F32)
        sp = jnp.where(z > SOFTPLUS_LINEAR_ABOVE, z, jnp.log2(1.0 + jnp.exp2(z)))
        log_beta = z - sp
        if diag:
            row = lax.broadcasted_iota(jnp.int32, z.shape, 0) % n
            col = lax.broadcasted_iota(jnp.int32, z.shape, 1)
            causal = col < row
            sp = jnp.where(causal, sp, 0.0)
        cin = jnp.dot(sp.astype(_BF16), neg_suffix, preferred_element_type=_F32)
        logw = log_beta + cin
        if carry is not None:
            logw = logw + jnp.concatenate([carry] * (tq // V7X_LANES), axis=1)
        a = jnp.exp2(logw)
        if diag:
            a = jnp.where(causal, a, 0.0)
        a = a.astype(_BF16)
        a_cat = jnp.concatenate([a[hh * n:(hh + 1) * n] for hh in range(nh)], axis=1)
        v_cat = jnp.concatenate(
            [jnp.where(_head_lanes(width, hh), vj, jnp.zeros_like(vj)) for hh in range(nh)], axis=0)
        pv = jnp.dot(a_cat, v_cat, preferred_element_type=_F32)
        total = jnp.broadcast_to(cin[:, 0:1] - sp[:, 0:1], (nh * n, V7X_LANES))
        return pv, (total if carry is None else carry + total)

    def low_rows_slack(slack):
        row = lax.broadcasted_iota(jnp.int32, slack.shape, 0) % tq
        return jnp.max(jnp.where(row >= top, slack, -jnp.inf))

    for qi in range(nq):
        rows = slice(qi * tq, (qi + 1) * tq)
        q = q_ref[rows, :]
        qs = head_stack(q)
        acc, carry = key_block(qs, None, qi, True)
        if qi >= 1:
            pv, carry = key_block(qs, carry, qi - 1, False)
            acc = acc + pv
        if qi >= 2:
            reach = reach_of(q)
            carry_top = top_rows(carry)
            pv, carry_top2 = key_block(top_rows(qs), carry_top, qi - 2, False)
            slack = carry + reach
            slack_low = low_rows_slack(slack)
            low_dead = slack_low <= DEAD_LOG2_WEIGHT
            acc = jnp.concatenate([acc[0:top] + jnp.where(low_dead, pv, 0.0), acc[top:]], axis=0)
            carry_top2 = jnp.where(low_dead, carry_top2, carry_top)
            acc_ref[qi] = acc
            carry_ref[qi] = carry
            for hh in range(nh):
                carry_ref[qi, hh * tq:hh * tq + top, :] = carry_top2[hh * top:(hh + 1) * top]
            slack_ref[qi, 0] = slack_low
            slack_ref[qi, 1] = jnp.max(carry_top2 + top_rows(reach))
            slack_ref[qi, 2] = jnp.max(slack)
        o_ref[rows, :] = acc.astype(_BF16)

    def finish(qi, c):
        low_dead = slack_ref[qi, 0] <= DEAD_LOG2_WEIGHT
        slack_top = slack_ref[qi, 1]
        slack_all = slack_ref[qi, 2]
        more = jnp.where(low_dead, (slack_top > DEAD_LOG2_WEIGHT) & (qi > 2),
                         slack_all > DEAD_LOG2_WEIGHT)

        @pl.when(more)
        def _visit_more():
            rows = pl.ds(pl.multiple_of(qi * tq, tq), tq)
            q = q_ref[rows, :]
            qs_ref[...] = head_stack(q)
            reach_ref[...] = reach_of(q)

            def top_alive(state):
                step, slack = state
                return (step < qi) & (slack > DEAD_LOG2_WEIGHT) & low_dead

            def top_step(state):
                step, _ = state
                pv, carry = key_block(top_rows(qs_ref), top_rows(carry_ref.at[qi]),
                                      qi - 1 - step, False)
                acc_ref[qi, 0:top, :] += pv
                for hh in range(nh):
                    carry_ref[qi, hh * tq:hh * tq + top, :] = carry[hh * top:(hh + 1) * top]
                return step + 1, jnp.max(carry + top_rows(reach_ref))

            lax.while_loop(top_alive, top_step, (jnp.int32(2), slack_top))

            def all_alive(state):
                step, slack = state
                return (step < qi) & (slack > DEAD_LOG2_WEIGHT) & jnp.logical_not(low_dead)

            def all_step(state):
                step, _ = state
                pv, carry = key_block(qs_ref[...], carry_ref[qi], qi - 1 - step, False)
                acc_ref[qi] += pv
                carry_ref[qi] = carry
                return step + 1, jnp.max(carry + reach_ref[...])

            lax.while_loop(all_alive, all_step, (jnp.int32(1), slack_all))
            o_ref[rows, :] = acc_ref[qi].astype(_BF16)

        return c

    lax.fori_loop(2, nq, finish, 0)


def _stick_attention(qkv, batch, seq):
    t, d3 = qkv.shape
    d = d3 // 3
    tq = ATTN_ROWS
    nh = HEADS_PER_STEP
    width = nh * HEAD_DIM
    nq = seq // tq
    q_spec, k_spec, v_spec, o_spec, hgs = _attn_specs(seq, d)
    est = (8 * seq * width * 2 + nh * tq * width * 2 + nq * tq * width * 4
           + (nq + 1) * nh * tq * V7X_LANES * 4 + 16 * nh * tq * tq * 4)
    return pl.pallas_call(
        functools.partial(_stick_attn_body, nq=seq // tq),
        grid=(hgs, batch),
        in_specs=[q_spec, k_spec, v_spec],
        out_specs=o_spec,
        out_shape=jax.ShapeDtypeStruct((t, d), _BF16),
        scratch_shapes=[
            pltpu.VMEM((nh * tq, width), _BF16),
            pltpu.VMEM((nq, tq, width), _F32),
            pltpu.VMEM((nq, nh * tq, V7X_LANES), _F32),
            pltpu.VMEM((nh * tq, V7X_LANES), _F32),
            pltpu.SMEM((nq, 3), _F32),
        ],
        compiler_params=pltpu.CompilerParams(
            dimension_semantics=("arbitrary", "arbitrary"),
            vmem_limit_bytes=_vmem_limit(est)),
        name="stick_attention",
    )(qkv, qkv, qkv)


def _post_body(o_ref, x_ref, wo_ref, g_post_mix_ref, g_pre_ffn_ref, wg_ref, wu_ref, wd_ref,
               g_post_ffn_ref, out_ref):
    m = jnp.dot(o_ref[...], wo_ref[...], preferred_element_type=_F32)
    x1 = x_ref[...] + _rms_norm(m, g_post_mix_ref[...])
    h = _rms_norm(x1, g_pre_ffn_ref[...]).astype(_BF16)
    gate = jnp.dot(h, wg_ref[...], preferred_element_type=_F32)
    up = jnp.dot(h, wu_ref[...], preferred_element_type=_F32)
    act = (gate * jax.nn.sigmoid(gate) * up).astype(_BF16)
    y = jnp.dot(act, wd_ref[...], preferred_element_type=_F32)
    out_ref[...] = x1 + _rms_norm(y, g_post_ffn_ref[...])


def _post_mix_ffn(o, x2, wo, g_post_mix, g_pre_ffn, wg, wu, wd, g_post_ffn, layer):
    t, d = x2.shape
    f = wg.shape[2]
    tm = TOKEN_ROWS
    est = (d * d + 3 * d * f) * 2 + 2 * tm * d * (2 + 4 + 4) + tm * (3 * d * 4 + 2 * f * 4 + f * 2 + d * 2)
    row = lambda i: (i, 0)
    return pl.pallas_call(
        _post_body,
        grid=(t // tm,),
        in_specs=[
            pl.BlockSpec((tm, d), row),
            pl.BlockSpec((tm, d), row),
            _resident((d, d), layer),
            _resident((1, d), layer),
            _resident((1, d), layer),
            _resident((d, f), layer),
            _resident((d, f), layer),
            _resident((f, d), layer),
            _resident((1, d), layer),
        ],
        out_specs=pl.BlockSpec((tm, d), row),
        out_shape=jax.ShapeDtypeStruct((t, d), _F32),
        compiler_params=pltpu.CompilerParams(
            dimension_semantics=("arbitrary",), vmem_limit_bytes=_vmem_limit(est)),
        name="post_mix_ffn",
    )(o, x2, wo, g_post_mix, g_pre_ffn, wg, wu, wd, g_post_ffn)


def kernel(x, g_pre_mix, g_post_mix, w_qkv, w_o, rel_bias, g_pre_ffn, g_post_ffn, w_gate, w_up, w_down):
    batch, seq, d = x.shape
    depth = w_qkv.shape[0]
    assert d % (HEADS_PER_STEP * HEAD_DIM) == 0 and seq % ATTN_ROWS == 0
    assert (LEFT_CHUNKS * CHUNK) % ATTN_ROWS == 0 and seq >= LEFT_CHUNKS * CHUNK + ATTN_ROWS
    assert (batch * seq) % TOKEN_ROWS == 0

    q_scale = LOG2E / math.sqrt(HEAD_DIM)
    col_scale = jnp.concatenate(
        [jnp.full((1, d), q_scale, _F32), jnp.ones((1, 2 * d), _F32)], axis=1)
    gains = lambda g: g.reshape(depth, 1, d).astype(_F32)
    g_pre_mix, g_post_mix, g_pre_ffn, g_post_ffn = map(
        gains, (g_pre_mix, g_post_mix, g_pre_ffn, g_post_ffn))
    w_qkv, w_o, w_gate, w_up, w_down = (
        w.astype(_BF16) for w in (w_qkv, w_o, w_gate, w_up, w_down))

    x2 = x.reshape(batch * seq, d)
    for i in range(depth):
        qkv = _qkv_proj(x2, g_pre_mix, w_qkv, col_scale, i)
        if i % 2 == 0:
            o = _chunk_attention(qkv, rel_bias[i // 2].astype(_F32), batch, seq)
        else:
            o = _stick_attention(qkv, batch, seq)
        x2 = _post_mix_ffn(o, x2, w_o, g_post_mix, g_pre_ffn, w_gate, w_up, w_down, g_post_ffn, i)
    return x2.reshape(batch, seq, d)
```

```python
import functools
import math

import jax
import jax.numpy as jnp
from jax import lax
from jax.experimental import pallas as pl
from jax.experimental.pallas import tpu as pltpu

HEAD_DIM = 64
CHUNK = 64
LEFT_CHUNKS = 8
REL_CLIP = 128
RMS_EPS = 1e-6
LOG2E = math.log2(math.e)
MASKED_LOGIT = -1e30
DEAD_LOG2_WEIGHT = -140.0
SOFTPLUS_LINEAR_ABOVE = 100.0

V7X_LANES = 128
V7X_MXU_DIM = 256
V7X_VMEM_BYTES = 64 * 1024 * 1024

HEADS_PER_STEP = V7X_MXU_DIM // HEAD_DIM
ATTN_ROWS = V7X_MXU_DIM
STICK_TOP_ROWS = ATTN_ROWS // 2
TOKEN_ROWS = 1024

_F32 = jnp.float32
_BF16 = jnp.bfloat16
_CONTRACT_LAST = (((1,), (1,)), ((), ()))


def _vmem_limit(estimate_bytes):
    return int(min(estimate_bytes * 3 // 2, V7X_VMEM_BYTES * 15 // 16))


def _rms_norm(x, g):
    ms = jnp.mean(x * x, axis=-1, keepdims=True)
    return x * lax.rsqrt(ms + RMS_EPS) * g


def _resident(shape, layer=None):
    if layer is None:
        return pl.BlockSpec(shape, lambda *_: (0,) * len(shape), pipeline_mode=pl.Buffered(1))
    return pl.BlockSpec((None,) + shape, lambda *_: (layer,) + (0,) * len(shape),
                        pipeline_mode=pl.Buffered(1))


def _qkv_body(x_ref, g_ref, w_ref, s_ref, o_ref):
    h = _rms_norm(x_ref[...], g_ref[...]).astype(_BF16)
    y = jnp.dot(h, w_ref[...], preferred_element_type=_F32)
    o_ref[...] = (y * s_ref[...]).astype(_BF16)


def _qkv_proj(x2, g, w, col_scale, layer):
    t, d = x2.shape
    n = w.shape[2]
    tm = TOKEN_ROWS
    est = 2 * tm * d * 4 + d * n * 2 + 2 * tm * n * 2 + tm * n * 4 + tm * d * 6
    return pl.pallas_call(
        _qkv_body,
        grid=(t // tm,),
        in_specs=[
            pl.BlockSpec((tm, d), lambda i: (i, 0)),
            _resident((1, d), layer),
            _resident((d, n), layer),
            _resident((1, n)),
        ],
        out_specs=pl.BlockSpec((tm, n), lambda i: (i, 0)),
        out_shape=jax.ShapeDtypeStruct((t, n), _BF16),
        compiler_params=pltpu.CompilerParams(
            dimension_semantics=("arbitrary",), vmem_limit_bytes=_vmem_limit(est)),
        name="qkv_proj",
    )(x2, g, w, col_scale)


def _head_lanes(width, hh):
    lane = lax.broadcasted_iota(jnp.int32, (1, width), 1)
    return (lane >= hh * HEAD_DIM) & (lane < (hh + 1) * HEAD_DIM)


def _attn_specs(seq, d):
    width = HEADS_PER_STEP * HEAD_DIM
    col_blocks = d // width
    q_spec = pl.BlockSpec((seq, width), lambda hg, b: (b, hg))
    k_spec = pl.BlockSpec((seq, width), lambda hg, b: (b, col_blocks + hg))
    v_spec = pl.BlockSpec((seq, width), lambda hg, b: (b, 2 * col_blocks + hg))
    o_spec = pl.BlockSpec((seq, width), lambda hg, b: (b, hg))
    return q_spec, k_spec, v_spec, o_spec, col_blocks


def _chunk_attn_body(rel_ref, q_ref, k_ref, v_ref, o_ref, bias_ref, *, n_rel, nq):
    hg, b = pl.program_id(0), pl.program_id(1)
    tq = ATTN_ROWS
    nh = HEADS_PER_STEP
    width = HEADS_PER_STEP * HEAD_DIM
    left = LEFT_CHUNKS * CHUNK
    left_blocks = left // tq
    band_blocks = left_blocks + 1
    master_blocks = band_blocks + left_blocks
    master = master_blocks * tq
    roll_w = master + tq

    @pl.when(b == 0)
    def _build_bias():
        wpos = lax.broadcasted_iota(jnp.int32, (8, roll_w), 1)
        ridx = jnp.clip(left + tq - wpos, -(CHUNK - 1), REL_CLIP) + (CHUNK - 1)
        i = lax.broadcasted_iota(jnp.int32, (tq, master), 0)
        u = lax.broadcasted_iota(jnp.int32, (tq, master), 1)
        d = left + i - u
        i_in_chunk = i % CHUNK
        visible = (d >= i_in_chunk - (CHUNK - 1)) & (d <= i_in_chunk + left)
        for hh in range(HEADS_PER_STEP):
            h = hg * HEADS_PER_STEP + hh

            def pick(r, base, h=h):
                return jnp.where(ridx == r, rel_ref[h, r], base)

            base = lax.fori_loop(0, n_rel, pick, jnp.zeros((8, roll_w), _F32))
            rows = jnp.broadcast_to(base[0:1, :] * LOG2E, (tq, roll_w))
            toeplitz = pltpu.roll(rows, master, 1, stride=1, stride_axis=0)
            tile = jnp.where(visible, toeplitz[:, :master], MASKED_LOGIT)
            for ub in range(master_blocks):
                bias_ref[hh, ub] = tile[:, ub * tq:(ub + 1) * tq]

    lane = lax.broadcasted_iota(jnp.int32, (1, width), 1)
    for qi in range(nq):
        start_blk = max(qi - left_blocks, 0)
        n_blocks = qi - start_blk + 1
        bias_blk0 = left_blocks - (qi - start_blk)
        kb = k_ref[start_blk * tq:(qi + 1) * tq, :]
        vb = v_ref[start_blk * tq:(qi + 1) * tq, :]
        q = q_ref[qi * tq:(qi + 1) * tq, :]
        out = None
        for hh in range(nh):
            qm = jnp.where(_head_lanes(width, hh), q, jnp.zeros_like(q))
            z = lax.dot_general(qm, kb, _CONTRACT_LAST, preferred_element_type=_F32)
            z = z + jnp.concatenate(
                [bias_ref[hh, bias_blk0 + t] for t in range(n_blocks)], axis=1)
            p = jnp.exp2(z - jnp.max(z, axis=-1, keepdims=True))
            den = jnp.sum(p, axis=-1, keepdims=True)
            o = jnp.dot(p.astype(_BF16), vb, preferred_element_type=_F32) / den
            out = o if out is None else jnp.where(lane >= hh * HEAD_DIM, o, out)
        o_ref[qi * tq:(qi + 1) * tq, :] = out.astype(_BF16)


def _chunk_attention(qkv, rel_bias, batch, seq):
    t, d3 = qkv.shape
    d = d3 // 3
    tq = ATTN_ROWS
    width = HEADS_PER_STEP * HEAD_DIM
    left_blocks = LEFT_CHUNKS * CHUNK // tq
    master_blocks = 2 * left_blocks + 1
    q_spec, k_spec, v_spec, o_spec, hgs = _attn_specs(seq, d)
    band = (left_blocks + 1) * tq
    bias_bytes = HEADS_PER_STEP * master_blocks * tq * tq * 4
    est = bias_bytes + 8 * seq * width * 2 + 2 * HEADS_PER_STEP * tq * band * 10
    return pl.pallas_call(
        functools.partial(_chunk_attn_body, n_rel=rel_bias.shape[1], nq=seq // tq),
        grid=(hgs, batch),
        in_specs=[pl.BlockSpec(memory_space=pltpu.SMEM), q_spec, k_spec, v_spec],
        out_specs=o_spec,
        out_shape=jax.ShapeDtypeStruct((t, d), _BF16),
        scratch_shapes=[pltpu.VMEM((HEADS_PER_STEP, master_blocks, tq, tq), _F32)],
        compiler_params=pltpu.CompilerParams(
            dimension_semantics=("arbitrary", "arbitrary"),
            vmem_limit_bytes=_vmem_limit(est)),
        name="chunk_attention",
    )(rel_bias, qkv, qkv, qkv)


def _head_sq_norms(x, head_cols):
    xf = x.astype(_F32)
    return jnp.dot((xf * xf).astype(_BF16), head_cols, preferred_element_type=_F32)


def _stick_attn_body(q_ref, k_ref, v_ref, o_ref, qs_ref, acc_ref, carry_ref, reach_ref, slack_ref,
                     *, nq):
    tq = ATTN_ROWS
    nh = HEADS_PER_STEP
    width = nh * HEAD_DIM
    top = STICK_TOP_ROWS

    r2 = lax.broadcasted_iota(jnp.int32, (tq, tq), 0)
    c2 = lax.broadcasted_iota(jnp.int32, (tq, tq), 1)
    neg_suffix = jnp.where(r2 > c2, -1.0, 0.0).astype(_BF16)

    hl = lax.broadcasted_iota(jnp.int32, (width, nh * V7X_LANES), 0) // HEAD_DIM
    hc = lax.broadcasted_iota(jnp.int32, (width, nh * V7X_LANES), 1) // V7X_LANES
    head_cols = jnp.where(hl == hc, 1.0, 0.0).astype(_BF16)

    k_sq_max = jnp.zeros((tq, nh * V7X_LANES), _F32)
    for j in range(nq):
        k_sq_max = jnp.maximum(k_sq_max, _head_sq_norms(k_ref[j * tq:(j + 1) * tq, :], head_cols))
    k_norm_max = jnp.sqrt(jnp.max(k_sq_max, axis=0, keepdims=True))

    def head_stack(q):
        return jnp.concatenate(
            [jnp.where(_head_lanes(width, hh), q, jnp.zeros_like(q)) for hh in range(nh)], axis=0)

    def top_rows(x):
        return jnp.concatenate([x[hh * tq:hh * tq + top] for hh in range(nh)], axis=0)

    def reach_of(q):
        reach = jnp.sqrt(_head_sq_norms(q, head_cols)) * (k_norm_max * 1.02) + 1.0
        return jnp.concatenate(
            [reach[:, hh * V7X_LANES:(hh + 1) * V7X_LANES] for hh in range(nh)], axis=0)

    def key_block(q_stack, carry, j, diag):
        n = q_stack.shape[0] // nh
        keys = (slice(j * tq, (j + 1) * tq) if isinstance(j, int)
                else pl.ds(pl.multiple_of(j * tq, tq), tq))
        kj = k_ref[keys, :]
        vj = v_ref[keys, :]
        z = lax.dot_general(q_stack, kj, _CONTRACT_LAST, preferred_element_type=_F32)
        sp = jnp.where(z > SOFTPLUS_LINEAR_ABOVE, z, jnp.log2(1.0 + jnp.exp2(z)))
        log_beta = z - sp
        if diag:
            row = lax.broadcasted_iota(jnp.int32, z.shape, 0) % n
            col = lax.broadcasted_iota(jnp.int32, z.shape, 1)
            causal = col < row
            sp = jnp.where(causal, sp, 0.0)
        cin = jnp.dot(sp.astype(_BF16), neg_suffix, preferred_element_type=_F32)
        logw = log_beta + cin
        if carry is not None:
            logw = logw + jnp.concatenate([carry] * (tq // V7X_LANES), axis=1)
        a = jnp.exp2(logw)
        if diag:
            a = jnp.where(causal, a, 0.0)
        a = a.astype(_BF16)
        a_cat = jnp.concatenate([a[hh * n:(hh + 1) * n] for hh in range(nh)], axis=1)
        v_cat = jnp.concatenate(
            [jnp.where(_head_lanes(width, hh), vj, jnp.zeros_like(vj)) for hh in range(nh)], axis=0)
        pv = jnp.dot(a_cat, v_cat, preferred_element_type=_F32)
        total = jnp.broadcast_to(cin[:, 0:1] - sp[:, 0:1], (nh * n, V7X_LANES))
        return pv, (total if carry is None else carry + total)

    for qi in range(nq):
        rows = slice(qi * tq, (qi + 1) * tq)
        q = q_ref[rows, :]
        qs = head_stack(q)
        acc, carry = key_block(qs, None, qi, True)
        if qi >= 1:
            pv, carry = key_block(qs, carry, qi - 1, False)
            acc = acc + pv
        if qi >= 2:
            slack = carry + reach_of(q)
            row = lax.broadcasted_iota(jnp.int32, slack.shape, 0) % tq
            acc_ref[qi] = acc
            carry_ref[qi] = carry
            slack_ref[qi, 0] = jnp.max(jnp.where(row >= top, slack, -jnp.inf))
            slack_ref[qi, 1] = jnp.max(jnp.where(row < top, slack, -jnp.inf))
        o_ref[rows, :] = acc.astype(_BF16)

    def finish(qi, c):
        slack_low = slack_ref[qi, 0]
        slack_top = slack_ref[qi, 1]
        low_dead = slack_low <= DEAD_LOG2_WEIGHT

        @pl.when(jnp.maximum(slack_low, slack_top) > DEAD_LOG2_WEIGHT)
        def _visit_more():
            rows = pl.ds(pl.multiple_of(qi * tq, tq), tq)
            q = q_ref[rows, :]
            qs_ref[...] = head_stack(q)
            reach_ref[...] = reach_of(q)

            def top_alive(state):
                step, slack = state
                return (step < qi) & (slack > DEAD_LOG2_WEIGHT) & low_dead

            def top_step(state):
                step, _ = state
                pv, carry = key_block(top_rows(qs_ref), top_rows(carry_ref.at[qi]),
                                      qi - 1 - step, False)
                acc_ref[qi, 0:top, :] += pv
                for hh in range(nh):
                    carry_ref[qi, hh * tq:hh * tq + top, :] = carry[hh * top:(hh + 1) * top]
                return step + 1, jnp.max(carry + top_rows(reach_ref))

            lax.while_loop(top_alive, top_step, (jnp.int32(1), slack_top))

            def all_alive(state):
                step, slack = state
                return (step < qi) & (slack > DEAD_LOG2_WEIGHT) & jnp.logical_not(low_dead)

            def all_step(state):
                step, _ = state
                pv, carry = key_block(qs_ref[...], carry_ref[qi], qi - 1 - step, False)
                acc_ref[qi] += pv
                carry_ref[qi] = carry
                return step + 1, jnp.max(carry + reach_ref[...])

            lax.while_loop(all_alive, all_step, (jnp.int32(1), jnp.maximum(slack_low, slack_top)))
            o_ref[rows, :] = acc_ref[qi].astype(_BF16)

        return c

    lax.fori_loop(2, nq, finish, 0)


def _stick_attention(qkv, batch, seq):
    t, d3 = qkv.shape
    d = d3 // 3
    tq = ATTN_ROWS
    nh = HEADS_PER_STEP
    width = nh * HEAD_DIM
    nq = seq // tq
    q_spec, k_spec, v_spec, o_spec, hgs = _attn_specs(seq, d)
    est = (8 * seq * width * 2 + nh * tq * width * 2 + nq * tq * width * 4
           + (nq + 1) * nh * tq * V7X_LANES * 4 + 16 * nh * tq * tq * 4)
    return pl.pallas_call(
        functools.partial(_stick_attn_body, nq=seq // tq),
        grid=(hgs, batch),
        in_specs=[q_spec, k_spec, v_spec],
        out_specs=o_spec,
        out_shape=jax.ShapeDtypeStruct((t, d), _BF16),
        scratch_shapes=[
            pltpu.VMEM((nh * tq, width), _BF16),
            pltpu.VMEM((nq, tq, width), _F32),
            pltpu.VMEM((nq, nh * tq, V7X_LANES), _F32),
            pltpu.VMEM((nh * tq, V7X_LANES), _F32),
            pltpu.SMEM((nq, 2), _F32),
        ],
        compiler_params=pltpu.CompilerParams(
            dimension_semantics=("arbitrary", "arbitrary"),
            vmem_limit_bytes=_vmem_limit(est)),
        name="stick_attention",
    )(qkv, qkv, qkv)


def _post_body(o_ref, x_ref, wo_ref, g_post_mix_ref, g_pre_ffn_ref, wg_ref, wu_ref, wd_ref,
               g_post_ffn_ref, out_ref):
    m = jnp.dot(o_ref[...], wo_ref[...], preferred_element_type=_F32)
    x1 = x_ref[...] + _rms_norm(m, g_post_mix_ref[...])
    h = _rms_norm(x1, g_pre_ffn_ref[...]).astype(_BF16)
    gate = jnp.dot(h, wg_ref[...], preferred_element_type=_F32)
    up = jnp.dot(h, wu_ref[...], preferred_element_type=_F32)
    act = (gate * jax.nn.sigmoid(gate) * up).astype(_BF16)
    y = jnp.dot(act, wd_ref[...], preferred_element_type=_F32)
    out_ref[...] = x1 + _rms_norm(y, g_post_ffn_ref[...])


def _post_mix_ffn(o, x2, wo, g_post_mix, g_pre_ffn, wg, wu, wd, g_post_ffn, layer):
    t, d = x2.shape
    f = wg.shape[2]
    tm = TOKEN_ROWS
    est = (d * d + 3 * d * f) * 2 + 2 * tm * d * (2 + 4 + 4) + tm * (3 * d * 4 + 2 * f * 4 + f * 2 + d * 2)
    row = lambda i: (i, 0)
    return pl.pallas_call(
        _post_body,
        grid=(t // tm,),
        in_specs=[
            pl.BlockSpec((tm, d), row),
            pl.BlockSpec((tm, d), row),
            _resident((d, d), layer),
            _resident((1, d), layer),
            _resident((1, d), layer),
            _resident((d, f), layer),
            _resident((d, f), layer),
            _resident((f, d), layer),
            _resident((1, d), layer),
        ],
        out_specs=pl.BlockSpec((tm, d), row),
        out_shape=jax.ShapeDtypeStruct((t, d), _F32),
        compiler_params=pltpu.CompilerParams(
            dimension_semantics=("arbitrary",), vmem_limit_bytes=_vmem_limit(est)),
        name="post_mix_ffn",
    )(o, x2, wo, g_post_mix, g_pre_ffn, wg, wu, wd, g_post_ffn)


def kernel(x, g_pre_mix, g_post_mix, w_qkv, w_o, rel_bias, g_pre_ffn, g_post_ffn, w_gate, w_up, w_down):
    batch, seq, d = x.shape
    depth = w_qkv.shape[0]
    assert d % (HEADS_PER_STEP * HEAD_DIM) == 0 and seq % ATTN_ROWS == 0
    assert (LEFT_CHUNKS * CHUNK) % ATTN_ROWS == 0 and seq >= LEFT_CHUNKS * CHUNK + ATTN_ROWS
    assert (batch * seq) % TOKEN_ROWS == 0

    q_scale = LOG2E / math.sqrt(HEAD_DIM)
    col_scale = jnp.concatenate(
        [jnp.full((1, d), q_scale, _F32), jnp.ones((1, 2 * d), _F32)], axis=1)
    gains = lambda g: g.reshape(depth, 1, d).astype(_F32)
    g_pre_mix, g_post_mix, g_pre_ffn, g_post_ffn = map(
        gains, (g_pre_mix, g_post_mix, g_pre_ffn, g_post_ffn))
    w_qkv, w_o, w_gate, w_up, w_down = (
        w.astype(_BF16) for w in (w_qkv, w_o, w_gate, w_up, w_down))

    x2 = x.reshape(batch * seq, d)
    for i in range(depth):
        qkv = _qkv_proj(x2, g_pre_mix, w_qkv, col_scale, i)
        if i % 2 == 0:
            o = _chunk_attention(qkv, rel_bias[i // 2].astype(_F32), batch, seq)
        else:
            o = _stick_attention(qkv, batch, seq)
        x2 = _post_mix_ffn(o, x2, w_o, g_post_mix, g_pre_ffn, w_gate, w_up, w_down, g_post_ffn, i)
    return x2.reshape(batch, seq, d)
```
